```python
import jax
import jax.numpy as jnp
from jax import lax
import numpy as np

D_MODEL = 1024
BATCH = 2
SEQ = 8192
DEPTH = 2
DEC_BATCH = 128
DEC_SEQ = 1
PAST_LEN = 16384
PAGE_SIZE = 128

N_A_LAYERS = (DEPTH + 1) // 2
N_C_LAYERS = DEPTH // 2
F32 = jnp.float32
EPS = 1e-6
NEG_INF = -1e30
ROPE_THETA = 500000.0
Q_BLOCK = 128
D_FF = 4 * D_MODEL

FOX_HEADS = 8
FOX_KV_HEADS = 4
FOX_HD = 64
FOX_GROUP = FOX_HEADS // FOX_KV_HEADS
FOX_SCALE = FOX_HD ** -0.5
FORGET_BIAS_MEAN = 3.0
MLA_HEADS = 8
MLA_Q_RANK = 384
MLA_KV_RANK = 256
MLA_NOPE = 64
MLA_ROPE = 32
MLA_V = 64
MLA_SCALE = (MLA_NOPE + MLA_ROPE) ** -0.5
A_SIZES = (FOX_HEADS * FOX_HD, FOX_KV_HEADS * FOX_HD, FOX_KV_HEADS * FOX_HD, FOX_HEADS,
           MLA_Q_RANK, MLA_KV_RANK, MLA_ROPE)
A_IN = (FOX_HEADS * FOX_HD + 2 * FOX_KV_HEADS * FOX_HD + FOX_HEADS
        + MLA_Q_RANK + MLA_KV_RANK + MLA_ROPE)
A_OUT = FOX_HEADS * FOX_HD + MLA_HEADS * MLA_V
C_HEADS = 16
C_HD = 64
C_ROT = C_HD // 4
C_SCALE = C_HD ** -0.5
C_BRANCHES = ((128, 1), (512, 4), (2048, 16))
C_MAX_WIN = 2048
C_IN = 3 * C_HEADS * C_HD
C_OUT = C_HEADS * C_HD

kernel_name = 'fox_mla_dilated_hybrid_step'


def _offsets(sizes):
    out, acc = [], 0
    for s in sizes[:-1]:
        acc += s
        out.append(acc)
    return out


def rmsnorm(x, g):
    xf = x.astype(F32)
    y = xf * lax.rsqrt(jnp.mean(xf * xf, axis=-1, keepdims=True) + EPS)
    return (y * g.astype(F32)).astype(x.dtype)


def rope(x, pos):
    half = x.shape[-1] // 2
    inv_freq = ROPE_THETA ** (-jnp.arange(half, dtype=F32) / half)
    ang = pos.astype(F32)[:, None] * inv_freq[None, :]
    cos = jnp.cos(ang)[:, None, :]
    sin = jnp.sin(ang)[:, None, :]
    xf = x.astype(F32)
    x1, x2 = xf[..., :half], xf[..., half:]
    return jnp.concatenate([x1 * cos - x2 * sin, x2 * cos + x1 * sin], axis=-1).astype(x.dtype)


def partial_rope(x, pos):
    return jnp.concatenate([rope(x[..., :C_ROT], pos), x[..., C_ROT:]], axis=-1)


def suffix_logf(logf):
    tail = lax.cumsum(logf[:, 1:], axis=1, reverse=True)
    return jnp.concatenate([tail, jnp.zeros_like(logf[:, :1])], axis=1)


def map_query_blocks(fn, qpos, *q_arrays):
    n_blk = qpos.shape[0] // Q_BLOCK

    def split(a):
        return jnp.moveaxis(a.reshape(a.shape[0], n_blk, Q_BLOCK, *a.shape[2:]), 1, 0)

    xs = (qpos.reshape(n_blk, Q_BLOCK),) + tuple(split(a) for a in q_arrays)
    out = lax.map(lambda b: fn(b[0], *b[1:]), xs)
    out = jnp.moveaxis(out, 0, 1)
    return out.reshape(out.shape[0], n_blk * Q_BLOCK, *out.shape[3:])


def gather_pages(cache, layer, page_table):
    g = cache[layer, page_table]
    return g.reshape(g.shape[0], g.shape[1] * g.shape[2], *g.shape[3:])


def fox_attend(qpos, q, rq, segs):
    B, Sq = q.shape[:2]
    rq_t = rq.reshape(B, Sq, FOX_KV_HEADS, FOX_GROUP).transpose(0, 2, 3, 1)[..., None]
    scores = []
    for k, v, rk, kpos in segs:
        s = jnp.einsum('bqkgd,bskd->bkgqs', q, k, preferred_element_type=F32) * FOX_SCALE
        rk_t = rk.reshape(B, k.shape[1], FOX_KV_HEADS, FOX_GROUP).transpose(0, 2, 3, 1)[:, :, :, None, :]
        mask = kpos[None, :] <= qpos[:, None]
        scores.append(jnp.where(mask, s + (rk_t - rq_t), NEG_INF))
    p = jax.nn.softmax(jnp.concatenate(scores, axis=-1), axis=-1)
    out, off = 0.0, 0
    for k, v, rk, kpos in segs:
        n = k.shape[1]
        out = out + jnp.einsum('bkgqs,bskd->bqkgd', p[..., off:off + n].astype(v.dtype), v)
        off += n
    return out


def mla_attend(qpos, q_lat, q_pe, segs):
    scores = []
    for c, kp, kpos in segs:
        s = (jnp.einsum('bqhr,bsr->bhqs', q_lat, c, preferred_element_type=F32)
             + jnp.einsum('bqhp,bsp->bhqs', q_pe, kp, preferred_element_type=F32)) * MLA_SCALE
        mask = kpos[None, :] <= qpos[:, None]
        scores.append(jnp.where(mask, s, NEG_INF))
    p = jax.nn.softmax(jnp.concatenate(scores, axis=-1), axis=-1)
    out, off = 0.0, 0
    for c, kp, kpos in segs:
        n = c.shape[1]
        out = out + jnp.einsum('bhqs,bsr->bqhr', p[..., off:off + n].astype(c.dtype), c)
        off += n
    return out


def proj_a(h, pos, w_in, fox_bf, mla_gq, mla_wqup, mla_gkv, mla_wuk):
    B, S, _ = h.shape
    z = jnp.einsum('bsd,de->bse', h, w_in)
    fq, fk, fv, ff, cq, ckv, kpe = jnp.split(z, _offsets(A_SIZES), axis=-1)
    fq = fq.reshape(B, S, FOX_KV_HEADS, FOX_GROUP, FOX_HD)
    fk = fk.reshape(B, S, FOX_KV_HEADS, FOX_HD)
    fv = fv.reshape(B, S, FOX_KV_HEADS, FOX_HD)
    logf = jax.nn.log_sigmoid((ff + fox_bf).astype(F32))
    qm = jnp.einsum('bsr,re->bse', rmsnorm(cq, mla_gq), mla_wqup)
    qm = qm.reshape(B, S, MLA_HEADS, MLA_NOPE + MLA_ROPE)
    q_lat = jnp.einsum('bshn,rhn->bshr', qm[..., :MLA_NOPE], mla_wuk)
    q_pe = rope(qm[..., MLA_NOPE:], pos)
    c_kv = rmsnorm(ckv, mla_gkv)
    k_pe = rope(kpe[:, :, None, :], pos)[:, :, 0]
    return fq, fk, fv, logf, q_lat, q_pe, c_kv, k_pe


def out_a(o_fox, o_lat, mla_wuv, w_out, dtype):
    B, S = o_fox.shape[:2]
    o_mla = jnp.einsum('bshr,rhv->bshv', o_lat, mla_wuv)
    o = jnp.concatenate([o_fox.reshape(B, S, -1), o_mla.reshape(B, S, -1)], axis=-1)
    return o.astype(dtype) @ w_out


def layer_a_prompt(h, pos, w_in, fox_bf, mla_gq, mla_wqup, mla_gkv, mla_wuk, mla_wuv, w_out):
    fq, fk, fv, logf, q_lat, q_pe, c_kv, k_pe = proj_a(h, pos, w_in, fox_bf, mla_gq, mla_wqup, mla_gkv, mla_wuk)
    rc = suffix_logf(logf)
    o_fox = map_query_blocks(lambda qp, q, rq: fox_attend(qp, q, rq, ((fk, fv, rc, pos),)), pos, fq, rc)
    o_lat = map_query_blocks(lambda qp, ql, qr: mla_attend(qp, ql, qr, ((c_kv, k_pe, pos),)), pos, q_lat, q_pe)
    return out_a(o_fox, o_lat, mla_wuv, w_out, h.dtype), (fk, fv, logf, c_kv, k_pe)


def layer_a_sample(h, pos, past, w_in, fox_bf, mla_gq, mla_wqup, mla_gkv, mla_wuk, mla_wuv, w_out):
    k_past, v_past, logf_past, ckv_past, kpe_past = past
    n_past = k_past.shape[1]
    past_pos = jnp.arange(n_past)
    fq, fk, fv, logf, q_lat, q_pe, c_kv, k_pe = proj_a(h, pos, w_in, fox_bf, mla_gq, mla_wqup, mla_gkv, mla_wuk)
    rc = suffix_logf(jnp.concatenate([logf_past.astype(F32), logf], axis=1))
    rc_past, rc_new = rc[:, :n_past], rc[:, n_past:]
    o_fox = fox_attend(pos, fq, rc_new, ((k_past, v_past, rc_past, past_pos), (fk, fv, rc_new, pos)))
    o_lat = mla_attend(pos, q_lat, q_pe, ((ckv_past, kpe_past, past_pos), (c_kv, k_pe, pos)))
    return out_a(o_fox, o_lat, mla_wuv, w_out, h.dtype), (fk, fv, logf, c_kv, k_pe)


def proj_c(h, pos, w_in):
    B, S, _ = h.shape
    z = jnp.einsum('bsd,de->bse', h, w_in).reshape(B, S, 3, C_HEADS, C_HD)
    return partial_rope(z[:, :, 0], pos), partial_rope(z[:, :, 1], pos), z[:, :, 2]


def dilated_branch_prompt(q, k, v, w, r):
    B, S, H, D = q.shape
    L = w // r
    per = r * L
    n_blk = -(-S // per)
    Sp = n_blk * per

    def to_blocks(a):
        a = jnp.pad(a, ((0, 0), (0, Sp - S), (0, 0), (0, 0)))
        a = a.reshape(B, Sp // r, r, H, D).transpose(0, 2, 1, 3, 4)
        return a.reshape(B, r, n_blk, L, H, D)

    def with_prev(a):
        prev = jnp.pad(a, ((0, 0), (0, 0), (1, 0), (0, 0), (0, 0), (0, 0)))[:, :, :-1]
        return jnp.concatenate([prev, a], axis=3)

    def from_blocks(a):
        tail = a.shape[4:]
        a = jnp.moveaxis(a.reshape(B, r, Sp // r, *tail), 1, 2)
        return a.reshape(B, Sp, *tail)[:, :S]

    qb = to_blocks(q)
    kk = with_prev(to_blocks(k))
    vv = with_prev(to_blocks(v))
    s = jnp.einsum('bcnqhd,bcnkhd->bcnhqk', qb, kk, preferred_element_type=F32) * C_SCALE
    i = jnp.arange(L)[:, None]
    j = jnp.arange(2 * L)[None, :]
    dist = L + i - j
    blk = jnp.arange(n_blk)[:, None, None]
    mask = (dist >= 0) & (dist <= L) & ((blk > 0) | (j >= L))
    s = jnp.where(mask[:, None], s, NEG_INF)
    mx = jnp.max(s, axis=-1)
    e = jnp.exp(s - mx[..., None])
    den = jnp.sum(e, axis=-1)
    num = jnp.einsum('bcnhqk,bcnkhd->bcnqhd', e, vv.astype(F32))
    return (from_blocks(num), from_blocks(jnp.swapaxes(den, -1, -2)),
            from_blocks(jnp.swapaxes(mx, -1, -2)))


def dilated_branch_sample(q, kc, vc, w, r):
    Sq, Lc = q.shape[1], kc.shape[1]
    L = w // r
    idx = (Lc - Sq + jnp.arange(Sq))[:, None] - r * jnp.arange(L + 1)[None, :]
    valid = idx >= 0
    idx = jnp.maximum(idx, 0)
    kg = kc[:, idx]
    vg = vc[:, idx]
    s = jnp.einsum('bqhd,bqkhd->bqhk', q, kg, preferred_element_type=F32) * C_SCALE
    s = jnp.where(valid[None, :, None, :], s, NEG_INF)
    mx = jnp.max(s, axis=-1)
    e = jnp.exp(s - mx[..., None])
    den = jnp.sum(e, axis=-1)
    num = jnp.einsum('bqhk,bqkhd->bqhd', e, vg.astype(F32))
    return num, den, mx


def merge_branches(parts):
    m = parts[0][2]
    for _, _, mx in parts[1:]:
        m = jnp.maximum(m, mx)
    num, den = 0.0, 0.0
    for n, d, mx in parts:
        a = jnp.exp(mx - m)
        num = num + n * a[..., None]
        den = den + d * a
    return num / den[..., None]


def layer_c_prompt(h, pos, w_in, w_out):
    B, S = h.shape[:2]
    q, k, v = proj_c(h, pos, w_in)
    o = merge_branches([dilated_branch_prompt(q, k, v, w, r) for w, r in C_BRANCHES])
    y = o.reshape(B, S, C_OUT).astype(h.dtype) @ w_out
    keep = min(C_MAX_WIN, S)
    return y, (k[:, S - keep:], v[:, S - keep:])


def layer_c_sample(h, pos, buf_k, buf_v, w_in, w_out):
    B, S = h.shape[:2]
    q, k, v = proj_c(h, pos, w_in)
    kc = jnp.concatenate([buf_k, k], axis=1)
    vc = jnp.concatenate([buf_v, v], axis=1)
    o = merge_branches([dilated_branch_sample(q, kc, vc, w, r) for w, r in C_BRANCHES])
    y = o.reshape(B, S, C_OUT).astype(h.dtype) @ w_out
    n = kc.shape[1]
    keep = min(C_MAX_WIN, PAST_LEN + S)
    return y, (kc[:, n - keep:], vc[:, n - keep:])


def sq_relu_mlp(h, w_up, w_down):
    return jnp.square(jax.nn.relu(h @ w_up)) @ w_down


def setup_inputs(seed: int = 0) -> dict:
    key = jax.random.key(seed)
    k = jax.random.split(key, 26)
    n_pages = PAST_LEN // PAGE_SIZE
    n_pool = (DEC_BATCH * n_pages * 5) // 4
    c_buf = min(C_MAX_WIN, PAST_LEN)

    def nrm(i, shape, scale=1.0):
        return jax.random.normal(k[i], shape, F32) * scale

    def gain(i, shape):
        return 1.0 + 0.02 * jax.random.normal(k[i], shape, F32)

    perm = jax.random.permutation(k[9], n_pool)
    page_table = perm[: DEC_BATCH * n_pages].reshape(DEC_BATCH, n_pages).astype(jnp.int32)
    return {
        'x_prompt': nrm(0, (BATCH, SEQ, D_MODEL)),
        'x_sample': nrm(1, (DEC_BATCH, DEC_SEQ, D_MODEL)),
        'cache_fox_k': nrm(2, (N_A_LAYERS, n_pool, PAGE_SIZE, FOX_KV_HEADS, FOX_HD)),
        'cache_fox_v': nrm(3, (N_A_LAYERS, n_pool, PAGE_SIZE, FOX_KV_HEADS, FOX_HD)),
        'cache_fox_logf': jax.nn.log_sigmoid(FORGET_BIAS_MEAN + nrm(4, (N_A_LAYERS, n_pool, PAGE_SIZE, FOX_HEADS))),
        'cache_mla_ckv': nrm(5, (N_A_LAYERS, n_pool, PAGE_SIZE, MLA_KV_RANK)),
        'cache_mla_kpe': nrm(6, (N_A_LAYERS, n_pool, PAGE_SIZE, MLA_ROPE)),
        'cache_win_k': nrm(7, (N_C_LAYERS, DEC_BATCH, c_buf, C_HEADS, C_HD)),
        'cache_win_v': nrm(8, (N_C_LAYERS, DEC_BATCH, c_buf, C_HEADS, C_HD)),
        'page_table': page_table,
        'a_w_in': nrm(10, (N_A_LAYERS, D_MODEL, A_IN), D_MODEL ** -0.5),
        'a_fox_bf': FORGET_BIAS_MEAN + 0.5 * nrm(11, (N_A_LAYERS, FOX_HEADS)),
        'a_mla_gq': gain(12, (N_A_LAYERS, MLA_Q_RANK)),
        'a_mla_wqup': nrm(13, (N_A_LAYERS, MLA_Q_RANK, MLA_HEADS * (MLA_NOPE + MLA_ROPE)), MLA_Q_RANK ** -0.5),
        'a_mla_gkv': gain(14, (N_A_LAYERS, MLA_KV_RANK)),
        'a_mla_wuk': nrm(15, (N_A_LAYERS, MLA_KV_RANK, MLA_HEADS, MLA_NOPE), MLA_KV_RANK ** -0.5),
        'a_mla_wuv': nrm(16, (N_A_LAYERS, MLA_KV_RANK, MLA_HEADS, MLA_V), MLA_KV_RANK ** -0.5),
        'a_w_out': nrm(17, (N_A_LAYERS, A_OUT, D_MODEL), A_OUT ** -0.5),
        'c_w_in': nrm(18, (N_C_LAYERS, D_MODEL, C_IN), D_MODEL ** -0.5),
        'c_w_out': nrm(19, (N_C_LAYERS, C_OUT, D_MODEL), C_OUT ** -0.5),
        'norm_mix': gain(20, (DEPTH, D_MODEL)),
        'norm_ffn': gain(21, (DEPTH, D_MODEL)),
        'ffn_w_up': nrm(22, (DEPTH, D_MODEL, D_FF), D_MODEL ** -0.5),
        'ffn_w_down': nrm(23, (DEPTH, D_FF, D_MODEL), D_FF ** -0.5),
        'norm_final': gain(24, (D_MODEL,)),
    }


def reference(x_prompt, x_sample, cache_fox_k, cache_fox_v, cache_fox_logf, cache_mla_ckv,
              cache_mla_kpe, cache_win_k, cache_win_v, page_table, a_w_in, a_fox_bf, a_mla_gq,
              a_mla_wqup, a_mla_gkv, a_mla_wuk, a_mla_wuv, a_w_out, c_w_in, c_w_out, norm_mix,
              norm_ffn, ffn_w_up, ffn_w_down, norm_final):
    pos_p = jnp.arange(x_prompt.shape[1])
    pos_s = PAST_LEN + jnp.arange(x_sample.shape[1])
    xp, xs = x_prompt, x_sample
    st_a_p, st_a_s, st_c_p, st_c_s = [], [], [], []
    for layer in range(DEPTH):
        li = layer // 2
        hp = rmsnorm(xp, norm_mix[layer])
        hs = rmsnorm(xs, norm_mix[layer])
        if layer % 2 == 0:
            wa = (a_w_in[li], a_fox_bf[li], a_mla_gq[li], a_mla_wqup[li], a_mla_gkv[li],
                  a_mla_wuk[li], a_mla_wuv[li], a_w_out[li])
            yp, st = layer_a_prompt(hp, pos_p, *wa)
            st_a_p.append(st)
            past = tuple(gather_pages(c, li, page_table) for c in
                         (cache_fox_k, cache_fox_v, cache_fox_logf, cache_mla_ckv, cache_mla_kpe))
            ys, st = layer_a_sample(hs, pos_s, past, *wa)
            st_a_s.append(st)
        else:
            yp, st = layer_c_prompt(hp, pos_p, c_w_in[li], c_w_out[li])
            st_c_p.append(st)
            ys, st = layer_c_sample(hs, pos_s, cache_win_k[li], cache_win_v[li], c_w_in[li], c_w_out[li])
            st_c_s.append(st)
        xp = xp + yp
        xs = xs + ys
        xp = xp + sq_relu_mlp(rmsnorm(xp, norm_ffn[layer]), ffn_w_up[layer], ffn_w_down[layer])
        xs = xs + sq_relu_mlp(rmsnorm(xs, norm_ffn[layer]), ffn_w_up[layer], ffn_w_down[layer])
    y_prompt = rmsnorm(xp, norm_final)
    y_sample = rmsnorm(xs, norm_final)

    def stack(states, j):
        return jnp.stack([s[j] for s in states])

    p_fox_k, p_fox_v, p_fox_logf = stack(st_a_p, 0), stack(st_a_p, 1), stack(st_a_p, 2)
    p_mla_ckv, p_mla_kpe = stack(st_a_p, 3), stack(st_a_p, 4)
    p_win_k, p_win_v = stack(st_c_p, 0), stack(st_c_p, 1)
    s_fox_k, s_fox_v, s_fox_logf = stack(st_a_s, 0), stack(st_a_s, 1), stack(st_a_s, 2)
    s_mla_ckv, s_mla_kpe = stack(st_a_s, 3), stack(st_a_s, 4)
    s_win_k, s_win_v = stack(st_c_s, 0), stack(st_c_s, 1)
    return (y_prompt, y_sample, p_fox_k, p_fox_v, p_fox_logf, p_mla_ckv, p_mla_kpe, p_win_k, p_win_v,
            s_fox_k, s_fox_v, s_fox_logf, s_mla_ckv, s_mla_kpe, s_win_k, s_win_v)
```

```python
import functools
import math

import numpy as np
import jax
import jax.numpy as jnp
from jax import lax
from jax.experimental import pallas as pl
from jax.experimental.pallas import tpu as pltpu

F32 = jnp.float32
BF16 = jnp.bfloat16
EPS = 1e-6
NEG_INF = -1e30
ROPE_THETA = 500000.0

LANES = 128
HEAD_DIM = 64
PAGE = 128

FOX_HEADS = 8
FOX_KV = 4
FOX_SCALE = HEAD_DIM ** -0.5
MLA_HEADS = 8
MLA_Q_RANK = 384
MLA_KV_RANK = 256
MLA_NOPE = 64
MLA_ROPE = 32
MLA_SCALE = (MLA_NOPE + MLA_ROPE) ** -0.5
C_HEADS = 16
C_ROT = 16
C_SCALE = HEAD_DIM ** -0.5
C_BRANCHES = ((128, 1), (512, 4), (2048, 16))
C_WIN = 2048

VMEM_LIMIT = 56 * 1024 * 1024


def _rms(x, g):
    return (x * lax.rsqrt(jnp.mean(x * x, axis=-1, keepdims=True) + EPS)) * g


def _dot(a, b):
    return jnp.dot(a, b, preferred_element_type=F32)


def _dot_nt(a, b):
    return lax.dot_general(a, b, (((1,), (1,)), ((), ())), preferred_element_type=F32)


def _const_spec(shape):
    n = len(shape)
    return pl.BlockSpec(shape, lambda *_: (0,) * n, pipeline_mode=pl.Buffered(1))


def _params(sem, limit=VMEM_LIMIT):
    return pltpu.CompilerParams(dimension_semantics=sem, vmem_limit_bytes=limit)


def _log_sigmoid(x):
    return jnp.minimum(x, 0.0) - jnp.log1p(jnp.exp(-jnp.abs(x)))


def _proj_a_kernel(x_ref, g_ref, wq_ref, wcq_ref, wckv_ref, wt_ref, bf_ref, gq_ref, wqup_ref,
                   gkv_ref, wukt_ref, wuvt_ref, cq_ref, s1_ref, s2_ref, ct_ref, st_ref,
                   qf_ref, qm_ref, kt_ref, vt_ref, lft_ref, ckv_ref, kpet_ref,
                   kfp_ref, vf2_ref, kmp_ref, vm_ref):
    h = _rms(x_ref[0], g_ref[...]).astype(BF16)
    qf_ref[0] = (_dot(h, wq_ref[...]) * FOX_SCALE).astype(BF16)

    cqn = _rms(_dot(h, wcq_ref[...]), gq_ref[...]).astype(BF16)
    qm = _dot(cqn, wqup_ref[...])
    cq, s1, s2 = cq_ref[...], s1_ref[...], s2_ref[...]
    for hh in range(MLA_HEADS):
        blk = qm[:, hh * LANES:(hh + 1) * LANES]
        rot = blk * cq + pltpu.roll(blk, LANES - 16, 1) * s1 + pltpu.roll(blk, 16, 1) * s2
        qm_ref[0, :, hh * LANES:(hh + 1) * LANES] = rot.astype(BF16)

    ckvn = _rms(_dot(h, wckv_ref[...]), gkv_ref[...])
    ckv_ref[0] = ckvn
    cb = ckvn.astype(BF16)
    knt = _dot_nt(wukt_ref[...], cb)
    vm_ref[0] = _dot_nt(wuvt_ref[...], cb).astype(BF16)

    zt = _dot_nt(wt_ref[...], h)
    kt = zt[0:256]
    vt = zt[256:512]
    kt_ref[0] = kt
    vt_ref[0] = vt
    x1, x2 = zt[512:528], zt[528:544]
    ct, st = ct_ref[...], st_ref[...]
    kpet = jnp.concatenate([x1 * ct - x2 * st, x2 * ct + x1 * st], axis=0)
    kpet_ref[0] = kpet
    lft_ref[0] = _log_sigmoid(zt[544:552] + bf_ref[...])

    tq = kt.shape[1]
    zero64 = jnp.zeros((HEAD_DIM, tq), BF16)
    ktb, vtb, kpb = kt.astype(BF16), vt.astype(BF16), kpet.astype(BF16)
    for hh in range(FOX_HEADS):
        g, a = hh // 2, hh % 2
        kg = ktb[g * HEAD_DIM:(g + 1) * HEAD_DIM]
        base = hh * LANES
        kfp_ref[0, base + a * HEAD_DIM:base + (a + 1) * HEAD_DIM, :] = kg
        kfp_ref[0, base + (1 - a) * HEAD_DIM:base + (2 - a) * HEAD_DIM, :] = zero64
    for g in range(FOX_KV):
        vg = vtb[g * HEAD_DIM:(g + 1) * HEAD_DIM]
        vf2_ref[0, g * LANES:g * LANES + HEAD_DIM, :] = vg
        vf2_ref[0, g * LANES + HEAD_DIM:(g + 1) * LANES, :] = vg
    zero32 = jnp.zeros((32, tq), BF16)
    kntb = knt.astype(BF16)
    for hh in range(MLA_HEADS):
        base = hh * LANES
        kmp_ref[0, base:base + MLA_NOPE, :] = kntb[hh * MLA_NOPE:(hh + 1) * MLA_NOPE]
        kmp_ref[0, base + MLA_NOPE:base + MLA_NOPE + MLA_ROPE, :] = kpb
        kmp_ref[0, base + MLA_NOPE + MLA_ROPE:base + LANES, :] = zero32


def _proj_a(x, wa, tabs, *, tq):
    B, S, D = x.shape
    cq, s1, s2, ct, st = tabs
    weights = (wa["g"], wa["wq"], wa["wcq"], wa["wckv"], wa["wt"], wa["bf"], wa["gq"], wa["wqup"],
               wa["gkv"], wa["wukt"], wa["wuvt"])
    in_specs = [pl.BlockSpec((1, tq, D), lambda b, i: (b, i, 0))]
    in_specs += [_const_spec(w.shape) for w in weights]
    in_specs += [pl.BlockSpec((tq, LANES), lambda b, i: (i, 0))] * 3
    in_specs += [pl.BlockSpec((16, tq), lambda b, i: (0, i))] * 2

    def nat(n, dt):
        return jax.ShapeDtypeStruct((B, S, n), dt), pl.BlockSpec((1, tq, n), lambda b, i: (b, i, 0))

    def trn(n, dt):
        return jax.ShapeDtypeStruct((B, n, S), dt), pl.BlockSpec((1, n, tq), lambda b, i: (b, 0, i))

    outs = [nat(512, BF16), nat(1024, BF16), trn(256, F32), trn(256, F32), trn(8, F32),
            nat(256, F32), trn(32, F32), trn(1024, BF16), trn(512, BF16), trn(1024, BF16),
            trn(512, BF16)]
    return pl.pallas_call(
        _proj_a_kernel,
        grid=(B, S // tq),
        in_specs=in_specs,
        out_specs=[o[1] for o in outs],
        out_shape=[o[0] for o in outs],
        compiler_params=_params(("parallel", "parallel")),
        name="proj_a",
    )(x, *weights, cq, s1, s2, ct, st)


def _lane_suffix_exclusive(x, lane):
    y = jnp.where(lane < LANES - 1, pltpu.roll(x, LANES - 1, 1), 0.0)
    for k in (1, 2, 4, 8, 16, 32, 64):
        y = y + jnp.where(lane < LANES - k, pltpu.roll(y, LANES - k, 1), 0.0)
    return y


def _suffix_kernel(lf_ref, rc_ref):
    nblk = lf_ref.shape[2] // LANES
    lane = lax.broadcasted_iota(jnp.int32, (8, LANES), 1)

    def body(t, carry):
        off = pl.multiple_of((nblk - 1 - t) * LANES, LANES)
        x = lf_ref[0, :, pl.ds(off, LANES)]
        y = _lane_suffix_exclusive(x, lane)
        rc_ref[0, :, pl.ds(off, LANES)] = y + carry
        return carry + (y[:, 0:1] + x[:, 0:1])

    lax.fori_loop(0, nblk, body, jnp.zeros((8, 1), F32))


def _suffix(lft):
    B, H, S = lft.shape
    return pl.pallas_call(
        _suffix_kernel,
        grid=(B,),
        in_specs=[pl.BlockSpec((1, H, S), lambda b: (b, 0, 0))],
        out_specs=pl.BlockSpec((1, H, S), lambda b: (b, 0, 0)),
        out_shape=jax.ShapeDtypeStruct((B, H, S), F32),
        compiler_params=_params(("parallel",)),
        name="suffix_logf",
    )(lft)


def _flash_kernel(*refs, tq, nb_win, has_bias, has_tab):
    q0_ref, q1_ref, k0_ref, k1_ref, v_ref = refs[:5]
    rest = list(refs[5:])
    b_ref = rest.pop(0) if has_bias else None
    tab_ref = rest.pop(0) if has_tab else None
    o_ref = rest.pop(0)
    i = pl.program_id(2)
    row = lax.broadcasted_iota(jnp.int32, (tq, tq), 0)
    col = lax.broadcasted_iota(jnp.int32, (tq, tq), 1)
    outs = []
    for a, (q_ref, k_ref) in enumerate(((q0_ref, k0_ref), (q1_ref, k1_ref))):
        q = q_ref[0]

        def block(j, d, carry, diag, a=a, q=q, k_ref=k_ref):
            m, l, acc = carry
            off = pl.multiple_of(j * tq, tq)
            s = _dot(q, k_ref[0, :, pl.ds(off, tq)])
            if has_bias:
                s = s + b_ref[0, a:a + 1, pl.ds(off, tq)]
            if has_tab:
                s = s + tab_ref[d]
            elif diag:
                s = jnp.where(col <= row, s, NEG_INF)
            m_new = jnp.maximum(m, jnp.max(s, axis=1, keepdims=True))
            alpha = jnp.exp(m - m_new)
            p = jnp.exp(s - m_new)
            l = alpha * l + jnp.sum(p, axis=1, keepdims=True)
            acc = alpha * acc + _dot_nt(p.astype(BF16), v_ref[0, :, pl.ds(off, tq)])
            return m_new, l, acc

        carry = (jnp.full((tq, 1), NEG_INF, F32), jnp.zeros((tq, 1), F32),
                 jnp.zeros((tq, LANES), F32))
        carry = block(i, 0, carry, True)
        n_prev = jnp.minimum(i, nb_win) if has_tab else i
        carry = lax.fori_loop(
            0, n_prev, lambda t, c, block=block: block(i - 1 - t, t + 1, c, False), carry)
        m, l, acc = carry
        outs.append(acc / l)
    lane = lax.broadcasted_iota(jnp.int32, (tq, LANES), 1)
    o_ref[0] = jnp.where(lane < HEAD_DIM, outs[0], outs[1]).astype(o_ref.dtype)


def _flash(q, kp, v2, bias, tab, *, n_pairs, q_per_head, tq, nb_win=0):
    B, S, _ = q.shape
    if q_per_head:
        qmaps = [lambda b, p, i: (b, i, 2 * p), lambda b, p, i: (b, i, 2 * p + 1)]
    else:
        qmaps = [lambda b, p, i: (b, i, p)] * 2
    in_specs = [pl.BlockSpec((1, tq, LANES), qmaps[0]), pl.BlockSpec((1, tq, LANES), qmaps[1]),
                pl.BlockSpec((1, LANES, S), lambda b, p, i: (b, 2 * p, 0)),
                pl.BlockSpec((1, LANES, S), lambda b, p, i: (b, 2 * p + 1, 0)),
                pl.BlockSpec((1, LANES, S), lambda b, p, i: (b, p, 0))]
    args = [q, q, kp, kp, v2]
    if bias is not None:
        in_specs.append(pl.BlockSpec((1, 2, S), lambda b, p, i: (b * n_pairs + p, 0, 0)))
        args.append(bias)
    if tab is not None:
        in_specs.append(_const_spec(tab.shape))
        args.append(tab)
    kern = functools.partial(_flash_kernel, tq=tq, nb_win=nb_win, has_bias=bias is not None,
                             has_tab=tab is not None)
    return pl.pallas_call(
        kern,
        grid=(B, n_pairs, S // tq),
        in_specs=in_specs,
        out_specs=pl.BlockSpec((1, tq, LANES), lambda b, p, i: (b, i, p)),
        out_shape=jax.ShapeDtypeStruct((B, S, n_pairs * LANES), BF16),
        compiler_params=_params(("parallel", "parallel", "arbitrary")),
        name="flash",
    )(*args)


def _out_mlp_kernel(*refs, n_parts, f_chunk, final):
    x_ref = refs[0]
    o_refs = refs[1:1 + n_parts]
    wo_ref, gf_ref, wup_ref, wdn_ref = refs[1 + n_parts:5 + n_parts]
    rest = list(refs[5 + n_parts:])
    gfin_ref = rest.pop(0) if final else None
    y_ref = rest.pop(0)
    proj = None
    r0 = 0
    for o_ref in o_refs:
        n = o_ref.shape[1]
        t = _dot(o_ref[...].astype(BF16), wo_ref[r0:r0 + n, :])
        proj = t if proj is None else proj + t
        r0 += n
    y1 = x_ref[...] + proj
    hn = _rms(y1, gf_ref[...]).astype(BF16)
    acc = y1
    d_ff = wup_ref.shape[1]
    for c in range(d_ff // f_chunk):
        u = _dot(hn, wup_ref[:, c * f_chunk:(c + 1) * f_chunk])
        a = jnp.square(jnp.maximum(u, 0.0)).astype(BF16)
        acc = acc + _dot(a, wdn_ref[c * f_chunk:(c + 1) * f_chunk, :])
    if final:
        acc = _rms(acc, gfin_ref[...])
    y_ref[...] = acc


def _out_mlp(x, o_parts, wo, gf, wup, wdn, gfin, *, tm):
    T, D = x.shape
    final = gfin is not None
    in_specs = [pl.BlockSpec((tm, D), lambda i: (i, 0))]
    in_specs += [pl.BlockSpec((tm, o.shape[1]), lambda i: (i, 0)) for o in o_parts]
    weights = [wo, gf, wup, wdn] + ([gfin] if final else [])
    in_specs += [_const_spec(w.shape) for w in weights]
    kern = functools.partial(_out_mlp_kernel, n_parts=len(o_parts), f_chunk=1024, final=final)
    return pl.pallas_call(
        kern,
        grid=(T // tm,),
        in_specs=in_specs,
        out_specs=pl.BlockSpec((tm, D), lambda i: (i, 0)),
        out_shape=jax.ShapeDtypeStruct((T, D), F32),
        compiler_params=_params(("parallel",)),
        name="out_mlp",
    )(x, *o_parts, *weights)


def _rope_rows(z, ct, st):
    x1, x2 = z[0:8], z[8:16]
    return jnp.concatenate([x1 * ct - x2 * st, x2 * ct + x1 * st, z[16:]], axis=0)


def _proj_c_prompt_kernel(x_ref, g_ref, wq_ref, wkvt_ref, cq_ref, s1_ref, s2_ref, ct_ref, st_ref,
                          q_ref, kp_ref, v_ref, ktf_ref, vtf_ref):
    h = _rms(x_ref[0], g_ref[...]).astype(BF16)
    zq = _dot(h, wq_ref[...])
    cq, s1, s2 = cq_ref[...], s1_ref[...], s2_ref[...]
    for cb in range(C_HEADS // 2):
        blk = zq[:, cb * LANES:(cb + 1) * LANES]
        rot = blk * cq + pltpu.roll(blk, LANES - 8, 1) * s1 + pltpu.roll(blk, 8, 1) * s2
        q_ref[0, :, cb * LANES:(cb + 1) * LANES] = rot.astype(BF16)
    zt = _dot_nt(wkvt_ref[...], h)
    n = C_HEADS * HEAD_DIM
    ct, st = ct_ref[...], st_ref[...]
    tq = zt.shape[1]
    zero64 = jnp.zeros((HEAD_DIM, tq), BF16)
    for hh in range(C_HEADS):
        kh = _rope_rows(zt[hh * HEAD_DIM:(hh + 1) * HEAD_DIM], ct, st)
        ktf_ref[0, hh * HEAD_DIM:(hh + 1) * HEAD_DIM, :] = kh
        a = hh % 2
        base = hh * LANES
        kp_ref[0, base + a * HEAD_DIM:base + (a + 1) * HEAD_DIM, :] = kh.astype(BF16)
        kp_ref[0, base + (1 - a) * HEAD_DIM:base + (2 - a) * HEAD_DIM, :] = zero64
    vt = zt[n:2 * n]
    vtf_ref[0] = vt
    v_ref[0] = vt.astype(BF16)


def _proj_c_prompt(x, g, wq, wkvt, tabs, *, tq):
    B, S, D = x.shape
    cq, s1, s2, ct, st = tabs
    n = C_HEADS * HEAD_DIM
    win_blocks = min(C_WIN, S) // tq
    first = S // tq - win_blocks
    weights = (g, wq, wkvt)
    in_specs = [pl.BlockSpec((1, tq, D), lambda b, i: (b, i, 0))]
    in_specs += [_const_spec(w.shape) for w in weights]
    in_specs += [pl.BlockSpec((tq, LANES), lambda b, i: (i, 0))] * 3
    in_specs += [pl.BlockSpec((8, tq), lambda b, i: (0, i))] * 2
    win_spec = pl.BlockSpec((1, n, tq), lambda b, i: (b, 0, jnp.maximum(i - first, 0)))
    return pl.pallas_call(
        _proj_c_prompt_kernel,
        grid=(B, S // tq),
        in_specs=in_specs,
        out_specs=[pl.BlockSpec((1, tq, n), lambda b, i: (b, i, 0)),
                   pl.BlockSpec((1, 2 * n, tq), lambda b, i: (b, 0, i)),
                   pl.BlockSpec((1, n, tq), lambda b, i: (b, 0, i)),
                   win_spec, win_spec],
        out_shape=[jax.ShapeDtypeStruct((B, S, n), BF16),
                   jax.ShapeDtypeStruct((B, 2 * n, S), BF16),
                   jax.ShapeDtypeStruct((B, n, S), BF16),
                   jax.ShapeDtypeStruct((B, n, win_blocks * tq), F32),
                   jax.ShapeDtypeStruct((B, n, win_blocks * tq), F32)],
        compiler_params=_params(("parallel", "arbitrary")),
        name="proj_c_prompt",
    )(x, *weights, cq, s1, s2, ct, st)


def _proj_c_sample_kernel(x_ref, g_ref, wt_ref, ct_ref, st_ref, qt_ref, kt_ref, vt_ref):
    h = _rms(x_ref[...], g_ref[...]).astype(BF16)
    zt = _dot_nt(wt_ref[...], h)
    n = C_HEADS * HEAD_DIM
    ct, st = ct_ref[...], st_ref[...]
    for hh in range(C_HEADS):
        sl = slice(hh * HEAD_DIM, (hh + 1) * HEAD_DIM)
        qt_ref[sl, :] = _rope_rows(zt[sl], ct, st) * C_SCALE
        kt_ref[sl, :] = _rope_rows(zt[n + hh * HEAD_DIM:n + (hh + 1) * HEAD_DIM], ct, st)
    vt_ref[...] = zt[2 * n:3 * n]


def _proj_c_sample(x, g, wt, ct, st):
    T, D = x.shape
    n = C_HEADS * HEAD_DIM
    vm = pl.BlockSpec(memory_space=pltpu.VMEM)
    return pl.pallas_call(
        _proj_c_sample_kernel,
        in_specs=[vm] * 5,
        out_specs=[vm] * 3,
        out_shape=[jax.ShapeDtypeStruct((n, T), F32)] * 3,
        compiler_params=pltpu.CompilerParams(vmem_limit_bytes=VMEM_LIMIT),
        name="proj_c_sample",
    )(x, g, wt, ct, st)


def _qlat_kernel(q_ref, w_ref, o_ref):
    for hh in range(MLA_HEADS):
        o_ref[:, hh * MLA_KV_RANK:(hh + 1) * MLA_KV_RANK] = _dot(
            q_ref[:, hh * LANES:(hh + 1) * LANES], w_ref[hh])


def _qlat(qm, wukp):
    T = qm.shape[0]
    vm = pl.BlockSpec(memory_space=pltpu.VMEM)
    return pl.pallas_call(
        _qlat_kernel, in_specs=[vm, vm], out_specs=vm,
        out_shape=jax.ShapeDtypeStruct((T, MLA_HEADS * MLA_KV_RANK), F32),
        name="mla_q_absorb",
    )(qm, wukp)


def _omla_kernel(o_ref, w_ref, y_ref):
    for p in range(MLA_HEADS // 2):
        acc = None
        for a in range(2):
            hh = 2 * p + a
            t = _dot(o_ref[:, hh * MLA_KV_RANK:(hh + 1) * MLA_KV_RANK].astype(BF16), w_ref[hh])
            acc = t if acc is None else acc + t
        y_ref[:, p * LANES:(p + 1) * LANES] = acc.astype(BF16)


def _omla(olat, wuvp):
    T = olat.shape[0]
    vm = pl.BlockSpec(memory_space=pltpu.VMEM)
    return pl.pallas_call(
        _omla_kernel, in_specs=[vm, vm], out_specs=vm,
        out_shape=jax.ShapeDtypeStruct((T, MLA_HEADS * HEAD_DIM), BF16),
        name="mla_v_expand",
    )(olat, wuvp)


def _decode_kernel(pt_ref, qbd_ref, ql_ref, qpe_ref, knew_ref, vnew_ref, cnew_ref, pnew_ref,
                   lfnew_ref, kt_hbm, vt_hbm, ckv_hbm, kpet_hbm, lft_hbm, of_ref, ol_ref,
                   kbuf, vbuf, cbuf, pbuf, lbuf, sem, *, n_pages, chunk):
    b = pl.program_id(0)
    nb = pl.num_programs(0)
    n_chunks = n_pages // chunk
    hbm = (kt_hbm, vt_hbm, ckv_hbm, kpet_hbm, lft_hbm)
    bufs = (kbuf, vbuf, cbuf, pbuf, lbuf)

    def copies(seq, c, slot):
        first = n_pages - (c + 1) * chunk
        out = []
        for pg in range(chunk):
            page = pt_ref[seq, first + pg]
            for kind in range(5):
                out.append(pltpu.make_async_copy(hbm[kind].at[page], bufs[kind].at[slot, pg],
                                                 sem.at[slot, kind]))
        return out

    def start(seq, c, slot):
        for cp in copies(seq, c, slot):
            cp.start()

    def wait(seq, c, slot):
        for cp in copies(seq, c, slot):
            cp.wait()

    @pl.when(b == 0)
    def _():
        start(0, 0, 0)

    qbd = qbd_ref[0].astype(BF16)
    ql = ql_ref[0].astype(BF16)
    qpe = qpe_ref[0]
    lane = lax.broadcasted_iota(jnp.int32, (8, LANES), 1)

    def process(c, slot, state):
        carry, mf, lf_, af, mm, lm, am = state

        @pl.when(c + 1 < n_chunks)
        def _():
            start(b, c + 1, 1 - slot)

        @pl.when(jnp.logical_and(c + 1 == n_chunks, b + 1 < nb))
        def _():
            start(b + 1, 0, 1 - slot)

        wait(b, c, slot)
        sf, sm, cps = [], [], []
        for pg in reversed(range(chunk)):
            x = lbuf[slot, pg]
            y = _lane_suffix_exclusive(x, lane)
            bias = y + carry
            carry = carry + (y[:, 0:1] + x[:, 0:1])
            sf.append(_dot(qbd, kbuf[slot, pg].astype(BF16)) + bias)
            cp = cbuf[slot, pg].astype(BF16)
            cps.append(cp)
            sm.append(_dot_nt(ql, cp) + _dot(qpe, pbuf[slot, pg].astype(BF16)))
        sf = jnp.concatenate(sf, axis=1)
        sm = jnp.concatenate(sm, axis=1)

        def soft(s, m, l):
            m_new = jnp.maximum(m, jnp.max(s, axis=1, keepdims=True))
            alpha = jnp.exp(m - m_new)
            p = jnp.exp(s - m_new)
            return m_new, alpha, alpha * l + jnp.sum(p, axis=1, keepdims=True), p.astype(BF16)

        mf, alf, lf_, pf = soft(sf, mf, lf_)
        mm, alm, lm, pm = soft(sm, mm, lm)
        af = af * alf
        am = am * alm
        for t, pg in enumerate(reversed(range(chunk))):
            af = af + _dot_nt(pf[:, t * LANES:(t + 1) * LANES], vbuf[slot, pg].astype(BF16))
            am = am + _dot(pm[:, t * LANES:(t + 1) * LANES], cps[t])
        return carry, mf, lf_, af, mm, lm, am

    neg = jnp.full((8, 1), NEG_INF, F32)
    zero1 = jnp.zeros((8, 1), F32)
    zacc = jnp.zeros((8, 2 * LANES), F32)
    state = (lfnew_ref[0], neg, zero1, zacc, neg, zero1, zacc)

    def pair(t, state):
        state = process(2 * t, 0, state)
        return process(2 * t + 1, 1, state)

    state = lax.fori_loop(0, n_chunks // 2, pair, state)
    _, mf, lf_, af, mm, lm, am = state

    qbd32, ql32 = qbd_ref[0], ql_ref[0]
    s_new = jnp.sum(qbd32 * knew_ref[0], axis=1, keepdims=True)
    m_fin = jnp.maximum(mf, s_new)
    al, p_new = jnp.exp(mf - m_fin), jnp.exp(s_new - m_fin)
    of_ref[0] = (af * al + p_new * vnew_ref[0]) / (lf_ * al + p_new)
    s_new = (jnp.sum(ql32 * cnew_ref[0], axis=1, keepdims=True)
             + jnp.sum(qpe.astype(F32) * pnew_ref[0], axis=1, keepdims=True))
    m_fin = jnp.maximum(mm, s_new)
    al, p_new = jnp.exp(mm - m_fin), jnp.exp(s_new - m_fin)
    ol_ref[0] = (am * al + p_new * cnew_ref[0]) / (lm * al + p_new)


def _decode(page_table, qbd, ql, qpe, knew, vnew, cnew, pnew, lfnew, kt_pages, vt_pages,
            ckv_pages, kpet_pages, lft_pages, *, chunk):
    nseq, n_pages = page_table.shape
    assert n_pages % (2 * chunk) == 0

    def seq_spec(shape):
        return pl.BlockSpec((1,) + shape, lambda b, pt: (b, 0, 0))

    anyspec = pl.BlockSpec(memory_space=pl.ANY)
    grid_spec = pltpu.PrefetchScalarGridSpec(
        num_scalar_prefetch=1,
        grid=(nseq,),
        in_specs=[seq_spec((8, 256)), seq_spec((8, 256)), seq_spec((8, MLA_ROPE)),
                  seq_spec((1, 256)), seq_spec((1, 256)), seq_spec((1, 256)),
                  seq_spec((1, MLA_ROPE)), seq_spec((8, 1))] + [anyspec] * 5,
        out_specs=[seq_spec((8, 256)), seq_spec((8, 256))],
        scratch_shapes=[pltpu.VMEM((2, chunk, 256, PAGE), F32),
                        pltpu.VMEM((2, chunk, 256, PAGE), F32),
                        pltpu.VMEM((2, chunk, PAGE, MLA_KV_RANK), F32),
                        pltpu.VMEM((2, chunk, MLA_ROPE, PAGE), F32),
                        pltpu.VMEM((2, chunk, FOX_HEADS, PAGE), F32),
                        pltpu.SemaphoreType.DMA((2, 5))],
    )
    kern = functools.partial(_decode_kernel, n_pages=n_pages, chunk=chunk)
    return pl.pallas_call(
        kern,
        grid_spec=grid_spec,
        out_shape=[jax.ShapeDtypeStruct((nseq, 8, 256), F32)] * 2,
        compiler_params=_params(("arbitrary",)),
        name="paged_decode",
    )(page_table, qbd, ql, qpe, knew, vnew, cnew, pnew, lfnew, kt_pages, vt_pages, ckv_pages,
      kpet_pages, lft_pages)


def _win_kernel(qt_ref, ktn_ref, vtn_ref, logc_ref, kin_ref, vin_ref, ot_ref, kout_ref, vout_ref,
                *, heads, log_self):
    b = pl.program_id(1)
    nseq = qt_ref.shape[1]
    sel = lax.broadcasted_iota(jnp.int32, (heads * HEAD_DIM, nseq), 1) == b

    def column(ref):
        return jnp.sum(jnp.where(sel, ref[...], 0.0), axis=1, keepdims=True)

    qcol, kcol, vcol = column(qt_ref), column(ktn_ref), column(vtn_ref)
    w = kin_ref.shape[3]
    lane = lax.broadcasted_iota(jnp.int32, (HEAD_DIM, w), 1)
    logc = logc_ref[...]
    cols = []
    for hh in range(heads):
        sl = slice(hh * HEAD_DIM, (hh + 1) * HEAD_DIM)
        q, kn, vn = qcol[sl], kcol[sl], vcol[sl]
        kin, vin = kin_ref[0, hh], vin_ref[0, hh]
        s = jnp.sum(kin * q, axis=0, keepdims=True) + logc
        s_new = jnp.sum(q * kn, axis=0, keepdims=True) + log_self
        m = jnp.maximum(jnp.max(s, axis=1, keepdims=True), s_new)
        p = jnp.exp(s - m)
        p_new = jnp.exp(s_new - m)
        l = jnp.sum(p, axis=1, keepdims=True) + p_new
        o = (jnp.sum(vin * p, axis=1, keepdims=True) + p_new * vn) / l
        cols.append(o)
        kout_ref[0, hh] = jnp.where(lane == w - 1, kn, pltpu.roll(kin, w - 1, 1))
        vout_ref[0, hh] = jnp.where(lane == w - 1, vn, pltpu.roll(vin, w - 1, 1))
    ocol = jnp.concatenate(cols, axis=0)

    @pl.when(b == 0)
    def _():
        ot_ref[...] = jnp.zeros_like(ot_ref)

    ot_ref[...] = jnp.where(sel, ocol, ot_ref[...])


def _win(qt, ktn, vtn, logc, kin, vin, *, heads, log_self):
    nseq, nh, d, w = kin.shape
    rows = heads * HEAD_DIM
    col_spec = pl.BlockSpec((rows, nseq), lambda hb, b: (hb, 0))
    buf_spec = pl.BlockSpec((1, heads, d, w), lambda hb, b: (b, hb, 0, 0))
    kern = functools.partial(_win_kernel, heads=heads, log_self=log_self)
    return pl.pallas_call(
        kern,
        grid=(nh // heads, nseq),
        in_specs=[col_spec, col_spec, col_spec, _const_spec(logc.shape), buf_spec, buf_spec],
        out_specs=[col_spec, buf_spec, buf_spec],
        out_shape=[jax.ShapeDtypeStruct((nh * d, nseq), F32),
                   jax.ShapeDtypeStruct(kin.shape, F32), jax.ShapeDtypeStruct(vin.shape, F32)],
        compiler_params=_params(("parallel", "arbitrary")),
        name="window_decode",
    )(qt, ktn, vtn, logc, kin, vin)


def _rope_tables_a(pos, scale):
    half = MLA_ROPE // 2
    inv = ROPE_THETA ** (-jnp.arange(half, dtype=F32) / half)
    ang = pos.astype(F32)[:, None] * inv[None, :]
    cos, sin = jnp.cos(ang), jnp.sin(ang)
    t = pos.shape[0]
    one, zero = jnp.ones((t, MLA_NOPE), F32), jnp.zeros((t, MLA_NOPE), F32)
    z16, z32 = jnp.zeros((t, half), F32), jnp.zeros((t, 32), F32)
    cq = jnp.concatenate([one, cos, cos, z32], axis=1) * scale
    s1 = jnp.concatenate([zero, -sin, z16, z32], axis=1) * scale
    s2 = jnp.concatenate([zero, z16, sin, z32], axis=1) * scale
    return cq, s1, s2, cos.T, sin.T


def _rope_tables_c(pos, scale):
    half = C_ROT // 2
    inv = ROPE_THETA ** (-jnp.arange(half, dtype=F32) / half)
    ang = pos.astype(F32)[:, None] * inv[None, :]
    cos, sin = jnp.cos(ang), jnp.sin(ang)
    t = pos.shape[0]
    one, zero = jnp.ones((t, HEAD_DIM - C_ROT), F32), jnp.zeros((t, HEAD_DIM - C_ROT), F32)
    z8 = jnp.zeros((t, half), F32)
    cq = jnp.concatenate([cos, cos, one] * 2, axis=1) * scale
    s1 = jnp.concatenate([-sin, z8, zero] * 2, axis=1) * scale
    s2 = jnp.concatenate([z8, sin, zero] * 2, axis=1) * scale
    return cq, s1, s2, cos.T, sin.T


def _band_table(tq, nb_win):
    d = np.arange(nb_win + 1)[:, None, None] * tq + np.arange(tq)[None, :, None] - np.arange(tq)[None, None, :]
    cnt = np.zeros(d.shape, np.float64)
    for w, r in C_BRANCHES:
        cnt += (d >= 0) & (d <= w) & (d % r == 0)
    return jnp.asarray(np.where(cnt > 0, np.log(np.maximum(cnt, 1.0)), NEG_INF), F32)


def _sample_logc(w):
    d = w - np.arange(w)
    cnt = np.zeros(d.shape, np.float64)
    for ww, r in C_BRANCHES:
        cnt += (d <= ww) & (d % r == 0)
    return jnp.asarray(np.where(cnt > 0, np.log(np.maximum(cnt, 1.0)), NEG_INF), F32)[None, :]


def _prep_a(w_in, fox_bf, gq, wqup, gkv, wuk, wuv, g_mix):
    o = np.cumsum((0, 512, 256, 256, 8, MLA_Q_RANK, MLA_KV_RANK, MLA_ROPE))
    wq, wk, wv, wf, wcq, wckv, wkpe = (w_in[:, o[t]:o[t + 1]] for t in range(7))
    wt = jnp.concatenate([wk, wv, wkpe, wf], axis=1).T.astype(BF16)
    up = wqup.reshape(MLA_Q_RANK, MLA_HEADS, MLA_NOPE + MLA_ROPE)
    up = jnp.pad(up, ((0, 0), (0, 0), (0, LANES - MLA_NOPE - MLA_ROPE)))
    wukt = jnp.transpose(wuk, (1, 2, 0)).reshape(MLA_HEADS * MLA_NOPE, MLA_KV_RANK)
    wuvt = jnp.transpose(wuv, (1, 2, 0)).reshape(MLA_HEADS * HEAD_DIM, MLA_KV_RANK)
    wukp = jnp.pad(jnp.transpose(wuk, (1, 2, 0)), ((0, 0), (0, LANES - MLA_NOPE), (0, 0)))
    wuv_h = jnp.transpose(wuv, (1, 0, 2))
    z = jnp.zeros_like(wuv_h)
    even = jnp.concatenate([wuv_h, z], axis=2)
    odd = jnp.concatenate([z, wuv_h], axis=2)
    wuvp = jnp.where((jnp.arange(MLA_HEADS) % 2 == 0)[:, None, None], even, odd)
    return dict(g=g_mix[None, :], wq=wq.astype(BF16), wcq=wcq.astype(BF16), wckv=wckv.astype(BF16),
                wt=wt, bf=fox_bf[:, None], gq=gq[None, :],
                wqup=up.reshape(MLA_Q_RANK, MLA_HEADS * LANES).astype(BF16), gkv=gkv[None, :],
                wukt=wukt.astype(BF16), wuvt=wuvt.astype(BF16), wukp=wukp.astype(BF16),
                wuvp=wuvp.astype(BF16))


def kernel(x_prompt, x_sample, cache_fox_k, cache_fox_v, cache_fox_logf, cache_mla_ckv, cache_mla_kpe, cache_win_k, cache_win_v, page_table, a_w_in, a_fox_bf, a_mla_gq, a_mla_wqup, a_mla_gkv, a_mla_wuk, a_mla_wuv, a_w_out, c_w_in, c_w_out, norm_mix, norm_ffn, ffn_w_up, ffn_w_down, norm_final):
    B, S, D = x_prompt.shape
    nseq = x_sample.shape[0]
    past_len = page_table.shape[1] * PAGE
    n_pool = cache_fox_k.shape[1]
    tq = 512
    pos_p = jnp.arange(S)
    pos_s = jnp.full((nseq,), past_len)

    wa = _prep_a(a_w_in[0], a_fox_bf[0], a_mla_gq[0], a_mla_wqup[0], a_mla_gkv[0], a_mla_wuk[0],
                 a_mla_wuv[0], norm_mix[0])
    wo_a = a_w_out[0].astype(BF16)
    gf0, wup0, wdn0 = norm_ffn[0][None, :], ffn_w_up[0].astype(BF16), ffn_w_down[0].astype(BF16)

    (qf, qm, kt, vt, lft, ckv, kpet, kfp, vf2, kmp, vm) = _proj_a(
        x_prompt, wa, _rope_tables_a(pos_p, MLA_SCALE), tq=tq)
    rct = _suffix(lft)
    o_fox = _flash(qf, kfp, vf2, rct.reshape(B * FOX_KV, 2, S), None, n_pairs=FOX_KV,
                   q_per_head=False, tq=tq)
    o_mla = _flash(qm, kmp, vm, None, None, n_pairs=MLA_HEADS // 2, q_per_head=True, tq=tq)
    xp = _out_mlp(x_prompt.reshape(B * S, D), [o_fox.reshape(B * S, -1), o_mla.reshape(B * S, -1)],
                  wo_a, gf0, wup0, wdn0, None, tm=512)

    p_fox_k = jnp.transpose(kt.reshape(B, FOX_KV, HEAD_DIM, S), (0, 3, 1, 2))[None]
    p_fox_v = jnp.transpose(vt.reshape(B, FOX_KV, HEAD_DIM, S), (0, 3, 1, 2))[None]
    p_fox_logf = jnp.transpose(lft, (0, 2, 1))[None]
    p_mla_ckv = ckv[None]
    p_mla_kpe = jnp.transpose(kpet, (0, 2, 1))[None]

    xs = x_sample.reshape(1, nseq, D)
    (qf_s, qm_s, kt_s, vt_s, lft_s, ckv_s, kpet_s, _, _, _, _) = _proj_a(
        xs, wa, _rope_tables_a(pos_s, MLA_SCALE), tq=nseq)
    qf_s = qf_s[0].astype(F32).reshape(nseq, FOX_KV, 2, HEAD_DIM)
    eye = jnp.eye(FOX_KV, dtype=F32)
    qbd = jnp.einsum("bkgd,kj->bkgjd", qf_s, eye).reshape(nseq, FOX_HEADS, FOX_KV * HEAD_DIM)
    ql = _qlat(qm_s[0], wa["wukp"]).reshape(nseq, MLA_HEADS, MLA_KV_RANK)
    qpe = qm_s[0].reshape(nseq, MLA_HEADS, LANES)[:, :, MLA_NOPE:MLA_NOPE + MLA_ROPE]
    knew = kt_s[0].T[:, None, :]
    vnew = vt_s[0].T[:, None, :]
    cnew = ckv_s[0][:, None, :]
    pnew = kpet_s[0].T[:, None, :]
    lfnew = lft_s[0].T[:, :, None]

    kt_pages = jnp.transpose(cache_fox_k[0], (0, 2, 3, 1)).reshape(n_pool, 256, PAGE)
    vt_pages = jnp.transpose(cache_fox_v[0], (0, 2, 3, 1)).reshape(n_pool, 256, PAGE)
    lft_pages = jnp.transpose(cache_fox_logf[0], (0, 2, 1))
    kpet_pages = jnp.transpose(cache_mla_kpe[0], (0, 2, 1))
    of_full, o_lat = _decode(page_table, qbd, ql, qpe, knew, vnew, cnew, pnew, lfnew, kt_pages,
                             vt_pages, cache_mla_ckv[0], kpet_pages, lft_pages, chunk=8)
    of_full = of_full.reshape(nseq, FOX_KV, 2, FOX_KV, HEAD_DIM)
    o_fox_s = jnp.einsum("bkgjd,kj->bkgd", of_full, eye).reshape(nseq, FOX_HEADS * HEAD_DIM)
    o_mla_s = _omla(o_lat.reshape(nseq, -1), wa["wuvp"])
    xs2 = _out_mlp(x_sample.reshape(nseq, D), [o_fox_s, o_mla_s], wo_a, gf0, wup0, wdn0, None,
                   tm=nseq)

    s_fox_k = kt_s[0].T.reshape(1, nseq, 1, FOX_KV, HEAD_DIM)
    s_fox_v = vt_s[0].T.reshape(1, nseq, 1, FOX_KV, HEAD_DIM)
    s_fox_logf = lft_s[0].T.reshape(1, nseq, 1, FOX_HEADS)
    s_mla_ckv = ckv_s.reshape(1, nseq, 1, MLA_KV_RANK)
    s_mla_kpe = kpet_s[0].T.reshape(1, nseq, 1, MLA_ROPE)

    n = C_HEADS * HEAD_DIM
    wc = c_w_in[0]
    wo_c = c_w_out[0].astype(BF16)
    g1 = norm_mix[1][None, :]
    gf1, wup1, wdn1 = norm_ffn[1][None, :], ffn_w_up[1].astype(BF16), ffn_w_down[1].astype(BF16)
    gfin = norm_final[None, :]
    nb_win = C_WIN // tq
    qc, kcp, vc, ktf, vtf = _proj_c_prompt(
        xp.reshape(B, S, D), g1, wc[:, :n].astype(BF16), wc[:, n:].T.astype(BF16),
        _rope_tables_c(pos_p, C_SCALE), tq=tq)
    o_c = _flash(qc, kcp, vc, None, _band_table(tq, nb_win), n_pairs=C_HEADS // 2,
                 q_per_head=False, tq=tq, nb_win=nb_win)
    y_prompt = _out_mlp(xp, [o_c.reshape(B * S, n)], wo_c, gf1, wup1, wdn1, gfin, tm=512)
    y_prompt = y_prompt.reshape(B, S, D)
    keep = ktf.shape[2]
    p_win_k = jnp.transpose(ktf.reshape(B, C_HEADS, HEAD_DIM, keep), (0, 3, 1, 2))[None]
    p_win_v = jnp.transpose(vtf.reshape(B, C_HEADS, HEAD_DIM, keep), (0, 3, 1, 2))[None]

    _, _, _, ct_s, st_s = _rope_tables_c(pos_s, 1.0)
    qt_s, ktn_s, vtn_s = _proj_c_sample(xs2, g1, wc.T.astype(BF16), ct_s, st_s)
    kin = jnp.transpose(cache_win_k[0], (0, 2, 3, 1))
    vin = jnp.transpose(cache_win_v[0], (0, 2, 3, 1))
    w = kin.shape[3]
    ot_s, kout, vout = _win(qt_s, ktn_s, vtn_s, _sample_logc(w), kin, vin, heads=4,
                            log_self=math.log(len(C_BRANCHES)))
    y_sample = _out_mlp(xs2, [ot_s.T], wo_c, gf1, wup1, wdn1, gfin, tm=nseq)
    y_sample = y_sample.reshape(nseq, 1, D)
    s_win_k = jnp.transpose(kout, (0, 3, 1, 2))[None]
    s_win_v = jnp.transpose(vout, (0, 3, 1, 2))[None]

    return (y_prompt, y_sample, p_fox_k, p_fox_v, p_fox_logf, p_mla_ckv, p_mla_kpe, p_win_k, p_win_v,
            s_fox_k, s_fox_v, s_fox_logf, s_mla_ckv, s_mla_kpe, s_win_k, s_win_v)
```

```python
import functools
import math

import numpy as np
import jax
import jax.numpy as jnp
from jax import lax
from jax.experimental import pallas as pl
from jax.experimental.pallas import tpu as pltpu

F32 = jnp.float32
BF16 = jnp.bfloat16
EPS = 1e-6
NEG_INF = -1e30
ROPE_THETA = 500000.0

LANES = 128
HEAD_DIM = 64
PAGE = 128

FOX_HEADS = 8
FOX_KV = 4
FOX_SCALE = HEAD_DIM ** -0.5
MLA_HEADS = 8
MLA_Q_RANK = 384
MLA_KV_RANK = 256
MLA_NOPE = 64
MLA_ROPE = 32
MLA_SCALE = (MLA_NOPE + MLA_ROPE) ** -0.5
C_HEADS = 16
C_ROT = 16
C_SCALE = HEAD_DIM ** -0.5
LOG2E = math.log2(math.e)
C_BRANCHES = ((128, 1), (512, 4), (2048, 16))
C_WIN = 2048

VMEM_LIMIT = 56 * 1024 * 1024

PROJ_TILE = 512
FLASH_TQ = 1024
FLASH_TK = 1024
DECODE_CHUNK = 16
WIN_HEADS = 8


def _rms(x, g):
    return (x * lax.rsqrt(jnp.mean(x * x, axis=-1, keepdims=True) + EPS)) * g


def _dot(a, b):
    return jnp.dot(a, b, preferred_element_type=F32)


def _dot_nt(a, b):
    return lax.dot_general(a, b, (((1,), (1,)), ((), ())), preferred_element_type=F32)


def _const_spec(shape):
    n = len(shape)
    return pl.BlockSpec(shape, lambda *_: (0,) * n, pipeline_mode=pl.Buffered(1))


def _params(sem, limit=VMEM_LIMIT):
    return pltpu.CompilerParams(dimension_semantics=sem, vmem_limit_bytes=limit)


def _store_value_slab(ref, base, v):
    t = v.shape[1]
    ones = jnp.where(lax.broadcasted_iota(jnp.int32, (16, t), 0) == 0, 1.0, 0.0).astype(BF16)
    ref[0, base:base + HEAD_DIM, :] = v
    ref[0, base + HEAD_DIM:base + HEAD_DIM + 16, :] = ones
    ref[0, base + HEAD_DIM + 16:base + LANES, :] = jnp.zeros((LANES - HEAD_DIM - 16, t), BF16)


def _log_sigmoid(x):
    return jnp.minimum(x, 0.0) - jnp.log1p(jnp.exp(-jnp.abs(x)))


def _proj_a_kernel(x_ref, g_ref, wq_ref, wcq_ref, wckv_ref, wt_ref, bf_ref, gq_ref, wqup_ref,
                   gkv_ref, wukt_ref, wuvt_ref, cq_ref, s1_ref, s2_ref, ct_ref, st_ref,
                   qf_ref, qm_ref, kt_ref, vt_ref, lft_ref, ckv_ref, kpet_ref,
                   kfp_ref, vf2_ref, kmp_ref, vm_ref):
    h = _rms(x_ref[0], g_ref[...]).astype(BF16)
    qf_ref[0] = (_dot(h, wq_ref[...]) * (FOX_SCALE * LOG2E)).astype(BF16)

    cqn = _rms(_dot(h, wcq_ref[...]), gq_ref[...]).astype(BF16)
    qm = _dot(cqn, wqup_ref[...])
    cq, s1, s2 = cq_ref[...], s1_ref[...], s2_ref[...]
    for hh in range(MLA_HEADS):
        blk = qm[:, hh * LANES:(hh + 1) * LANES]
        rot = blk * cq + pltpu.roll(blk, LANES - 16, 1) * s1 + pltpu.roll(blk, 16, 1) * s2
        qm_ref[0, :, hh * LANES:(hh + 1) * LANES] = rot.astype(BF16)

    ckvn = _rms(_dot(h, wckv_ref[...]), gkv_ref[...])
    ckv_ref[0] = ckvn
    cb = ckvn.astype(BF16)
    knt = _dot_nt(wukt_ref[...], cb)
    vmt = _dot_nt(wuvt_ref[...], cb).astype(BF16)
    for hh in range(MLA_HEADS):
        _store_value_slab(vm_ref, hh * LANES, vmt[hh * HEAD_DIM:(hh + 1) * HEAD_DIM])

    zt = _dot_nt(wt_ref[...], h)
    kt = zt[0:256]
    vt = zt[256:512]
    kt_ref[0] = kt
    vt_ref[0] = vt
    x1, x2 = zt[512:528], zt[528:544]
    ct, st = ct_ref[...], st_ref[...]
    kpet = jnp.concatenate([x1 * ct - x2 * st, x2 * ct + x1 * st], axis=0)
    kpet_ref[0] = kpet
    lft_ref[0] = _log_sigmoid(zt[544:552] + bf_ref[...])

    tq = kt.shape[1]
    zero64 = jnp.zeros((HEAD_DIM, tq), BF16)
    ktb, vtb, kpb = kt.astype(BF16), vt.astype(BF16), kpet.astype(BF16)
    for hh in range(FOX_HEADS):
        g, a = hh // 2, hh % 2
        kg = ktb[g * HEAD_DIM:(g + 1) * HEAD_DIM]
        base = hh * LANES
        kfp_ref[0, base + a * HEAD_DIM:base + (a + 1) * HEAD_DIM, :] = kg
        kfp_ref[0, base + (1 - a) * HEAD_DIM:base + (2 - a) * HEAD_DIM, :] = zero64
    for g in range(FOX_KV):
        _store_value_slab(vf2_ref, g * LANES, vtb[g * HEAD_DIM:(g + 1) * HEAD_DIM])
    zero32 = jnp.zeros((32, tq), BF16)
    kntb = knt.astype(BF16)
    for hh in range(MLA_HEADS):
        base = hh * LANES
        kmp_ref[0, base:base + MLA_NOPE, :] = kntb[hh * MLA_NOPE:(hh + 1) * MLA_NOPE]
        kmp_ref[0, base + MLA_NOPE:base + MLA_NOPE + MLA_ROPE, :] = kpb
        kmp_ref[0, base + MLA_NOPE + MLA_ROPE:base + LANES, :] = zero32


def _proj_a(x, wa, tabs, *, tq):
    B, S, D = x.shape
    cq, s1, s2, ct, st = tabs
    weights = (wa["g"], wa["wq"], wa["wcq"], wa["wckv"], wa["wt"], wa["bf"], wa["gq"], wa["wqup"],
               wa["gkv"], wa["wukt"], wa["wuvt"])
    in_specs = [pl.BlockSpec((1, tq, D), lambda b, i: (b, i, 0))]
    in_specs += [_const_spec(w.shape) for w in weights]
    in_specs += [pl.BlockSpec((tq, LANES), lambda b, i: (i, 0))] * 3
    in_specs += [pl.BlockSpec((16, tq), lambda b, i: (0, i))] * 2

    def nat(n, dt):
        return jax.ShapeDtypeStruct((B, S, n), dt), pl.BlockSpec((1, tq, n), lambda b, i: (b, i, 0))

    def trn(n, dt):
        return jax.ShapeDtypeStruct((B, n, S), dt), pl.BlockSpec((1, n, tq), lambda b, i: (b, 0, i))

    outs = [nat(512, BF16), nat(1024, BF16), trn(256, F32), trn(256, F32), trn(8, F32),
            nat(256, F32), trn(32, F32), trn(1024, BF16), trn(512, BF16), trn(1024, BF16),
            trn(1024, BF16)]
    return pl.pallas_call(
        _proj_a_kernel,
        grid=(B, S // tq),
        in_specs=in_specs,
        out_specs=[o[1] for o in outs],
        out_shape=[o[0] for o in outs],
        compiler_params=_params(("parallel", "parallel")),
        name="proj_a",
    )(x, *weights, cq, s1, s2, ct, st)


def _lane_suffix_exclusive(x, lane):
    y = jnp.where(lane < LANES - 1, pltpu.roll(x, LANES - 1, 1), 0.0)
    for k in (1, 2, 4, 8, 16, 32, 64):
        y = y + jnp.where(lane < LANES - k, pltpu.roll(y, LANES - k, 1), 0.0)
    return y


def _suffix_kernel(lf_ref, rc_ref):
    nblk = lf_ref.shape[2] // LANES
    lane = lax.broadcasted_iota(jnp.int32, (8, LANES), 1)

    def body(t, carry):
        off = pl.multiple_of((nblk - 1 - t) * LANES, LANES)
        x = lf_ref[0, :, pl.ds(off, LANES)]
        y = _lane_suffix_exclusive(x, lane)
        rc_ref[0, :, pl.ds(off, LANES)] = (y + carry) * LOG2E
        return carry + (y[:, 0:1] + x[:, 0:1])

    lax.fori_loop(0, nblk, body, jnp.zeros((8, 1), F32))


def _suffix(lft):
    B, H, S = lft.shape
    return pl.pallas_call(
        _suffix_kernel,
        grid=(B,),
        in_specs=[pl.BlockSpec((1, H, S), lambda b: (b, 0, 0))],
        out_specs=pl.BlockSpec((1, H, S), lambda b: (b, 0, 0)),
        out_shape=jax.ShapeDtypeStruct((B, H, S), F32),
        compiler_params=_params(("parallel",)),
        name="suffix_logf",
    )(lft)


def _flash_kernel(*refs, tq, tk, nb_win, has_bias, has_tab):
    q0_ref, q1_ref, k0_ref, k1_ref, v0_ref, v1_ref = refs[:6]
    rest = list(refs[6:])
    b_ref = rest.pop(0) if has_bias else None
    tab_ref = rest.pop(0) if has_tab else None
    o_ref = rest.pop(0)
    i = pl.program_id(2)
    r = tq // tk
    diff = (lax.broadcasted_iota(jnp.int32, (tq, tk), 1)
            - lax.broadcasted_iota(jnp.int32, (tq, tk), 0))
    qs = (q0_ref[0], q1_ref[0])
    k_refs = (k0_ref, k1_ref)
    v_refs = (v0_ref, v1_ref)

    def block(od, carry):
        off = pl.multiple_of((i * r - od) * tk, tk)
        out = []
        for a in range(2):
            m, acc = carry[a]
            s = _dot(qs[a], k_refs[a][0, :, pl.ds(off, tk)])
            if has_bias:
                s = s + b_ref[0, a:a + 1, pl.ds(off, tk)]
            if has_tab:
                s = s + tab_ref[od + (r - 1)]
            elif isinstance(od, int):
                s = jnp.where(diff <= od * tk, s, NEG_INF)
            m_new = jnp.maximum(m, jnp.max(s, axis=1, keepdims=True))
            alpha = jnp.exp2(m - m_new)
            p = jnp.exp2((s - m_new).astype(BF16))
            acc = alpha * acc + _dot_nt(p, v_refs[a][0, :, pl.ds(off, tk)])
            out.append((m_new, acc))
        return tuple(out)

    init = (jnp.full((tq, 1), NEG_INF, F32), jnp.zeros((tq, LANES), F32))
    carry = (init, init)
    for od in range(0, -r, -1):
        carry = block(od, carry)
    n_prev = jnp.minimum(i * r, nb_win) if has_tab else i * r
    carry = lax.fori_loop(0, n_prev, lambda t, c: block(t + 1, c), carry)
    outs = [acc / acc[:, HEAD_DIM:HEAD_DIM + 1] for (_, acc) in carry]
    lane = lax.broadcasted_iota(jnp.int32, (tq, LANES), 1)
    o_ref[0] = jnp.where(lane < HEAD_DIM, outs[0],
                         pltpu.roll(outs[1], HEAD_DIM, 1)).astype(o_ref.dtype)


def _flash(q, kp, v, bias, tab, *, n_pairs, q_per_head, v_per_head, tq, tk, nb_win=0):
    B, S, _ = q.shape
    per_head = [lambda b, p, i: (b, i, 2 * p), lambda b, p, i: (b, i, 2 * p + 1)]
    qmaps = per_head if q_per_head else [lambda b, p, i: (b, i, p)] * 2
    if v_per_head:
        vmaps = [lambda b, p, i: (b, 2 * p, 0), lambda b, p, i: (b, 2 * p + 1, 0)]
    else:
        vmaps = [lambda b, p, i: (b, p, 0)] * 2
    in_specs = [pl.BlockSpec((1, tq, LANES), qmaps[0]), pl.BlockSpec((1, tq, LANES), qmaps[1]),
                pl.BlockSpec((1, LANES, S), lambda b, p, i: (b, 2 * p, 0)),
                pl.BlockSpec((1, LANES, S), lambda b, p, i: (b, 2 * p + 1, 0)),
                pl.BlockSpec((1, LANES, S), vmaps[0]), pl.BlockSpec((1, LANES, S), vmaps[1])]
    args = [q, q, kp, kp, v, v]
    if bias is not None:
        in_specs.append(pl.BlockSpec((1, 2, S), lambda b, p, i: (b * n_pairs + p, 0, 0)))
        args.append(bias)
    if tab is not None:
        in_specs.append(_const_spec(tab.shape))
        args.append(tab)
    kern = functools.partial(_flash_kernel, tq=tq, tk=tk, nb_win=nb_win,
                             has_bias=bias is not None, has_tab=tab is not None)
    return pl.pallas_call(
        kern,
        grid=(B, n_pairs, S // tq),
        in_specs=in_specs,
        out_specs=pl.BlockSpec((1, tq, LANES), lambda b, p, i: (b, i, p)),
        out_shape=jax.ShapeDtypeStruct((B, S, n_pairs * LANES), BF16),
        compiler_params=_params(("parallel", "parallel", "arbitrary")),
        name="flash",
    )(*args)


def _out_mlp_kernel(*refs, n_parts, f_chunk, final):
    x_ref = refs[0]
    o_refs = refs[1:1 + n_parts]
    wo_ref, gf_ref, wup_ref, wdn_ref = refs[1 + n_parts:5 + n_parts]
    rest = list(refs[5 + n_parts:])
    gfin_ref = rest.pop(0) if final else None
    y_ref = rest.pop(0)
    proj = None
    r0 = 0
    for o_ref in o_refs:
        n = o_ref.shape[1]
        t = _dot(o_ref[...].astype(BF16), wo_ref[r0:r0 + n, :])
        proj = t if proj is None else proj + t
        r0 += n
    y1 = x_ref[...] + proj
    hn = _rms(y1, gf_ref[...]).astype(BF16)
    acc = y1
    d_ff = wup_ref.shape[1]
    for c in range(d_ff // f_chunk):
        u = _dot(hn, wup_ref[:, c * f_chunk:(c + 1) * f_chunk])
        a = jnp.square(jnp.maximum(u, 0.0)).astype(BF16)
        acc = acc + _dot(a, wdn_ref[c * f_chunk:(c + 1) * f_chunk, :])
    if final:
        acc = _rms(acc, gfin_ref[...])
    y_ref[...] = acc


def _out_mlp(x, o_parts, wo, gf, wup, wdn, gfin, *, tm):
    T, D = x.shape
    final = gfin is not None
    in_specs = [pl.BlockSpec((tm, D), lambda i: (i, 0))]
    in_specs += [pl.BlockSpec((tm, o.shape[1]), lambda i: (i, 0)) for o in o_parts]
    weights = [wo, gf, wup, wdn] + ([gfin] if final else [])
    in_specs += [_const_spec(w.shape) for w in weights]
    kern = functools.partial(_out_mlp_kernel, n_parts=len(o_parts), f_chunk=1024, final=final)
    return pl.pallas_call(
        kern,
        grid=(T // tm,),
        in_specs=in_specs,
        out_specs=pl.BlockSpec((tm, D), lambda i: (i, 0)),
        out_shape=jax.ShapeDtypeStruct((T, D), F32),
        compiler_params=_params(("parallel",)),
        name="out_mlp",
    )(x, *o_parts, *weights)


def _rope_rows(z, ct, st):
    x1, x2 = z[0:8], z[8:16]
    return jnp.concatenate([x1 * ct - x2 * st, x2 * ct + x1 * st, z[16:]], axis=0)


def _proj_c_prompt_kernel(x_ref, g_ref, wq_ref, wkvt_ref, cq_ref, s1_ref, s2_ref, ct_ref, st_ref,
                          q_ref, kp_ref, v_ref, ktf_ref, vtf_ref):
    h = _rms(x_ref[0], g_ref[...]).astype(BF16)
    zq = _dot(h, wq_ref[...])
    cq, s1, s2 = cq_ref[...], s1_ref[...], s2_ref[...]
    for cb in range(C_HEADS // 2):
        blk = zq[:, cb * LANES:(cb + 1) * LANES]
        rot = blk * cq + pltpu.roll(blk, LANES - 8, 1) * s1 + pltpu.roll(blk, 8, 1) * s2
        q_ref[0, :, cb * LANES:(cb + 1) * LANES] = rot.astype(BF16)
    zt = _dot_nt(wkvt_ref[...], h)
    n = C_HEADS * HEAD_DIM
    ct, st = ct_ref[...], st_ref[...]
    tq = zt.shape[1]
    zero64 = jnp.zeros((HEAD_DIM, tq), BF16)
    for hh in range(C_HEADS):
        kh = _rope_rows(zt[hh * HEAD_DIM:(hh + 1) * HEAD_DIM], ct, st)
        ktf_ref[0, hh * HEAD_DIM:(hh + 1) * HEAD_DIM, :] = kh
        a = hh % 2
        base = hh * LANES
        kp_ref[0, base + a * HEAD_DIM:base + (a + 1) * HEAD_DIM, :] = kh.astype(BF16)
        kp_ref[0, base + (1 - a) * HEAD_DIM:base + (2 - a) * HEAD_DIM, :] = zero64
    vt = zt[n:2 * n]
    vtf_ref[0] = vt
    vtb = vt.astype(BF16)
    for hh in range(C_HEADS):
        _store_value_slab(v_ref, hh * LANES, vtb[hh * HEAD_DIM:(hh + 1) * HEAD_DIM])


def _proj_c_prompt(x, g, wq, wkvt, tabs, *, tq):
    B, S, D = x.shape
    cq, s1, s2, ct, st = tabs
    n = C_HEADS * HEAD_DIM
    win_blocks = min(C_WIN, S) // tq
    first = S // tq - win_blocks
    weights = (g, wq, wkvt)
    in_specs = [pl.BlockSpec((1, tq, D), lambda b, i: (b, i, 0))]
    in_specs += [_const_spec(w.shape) for w in weights]
    in_specs += [pl.BlockSpec((tq, LANES), lambda b, i: (i, 0))] * 3
    in_specs += [pl.BlockSpec((8, tq), lambda b, i: (0, i))] * 2
    win_spec = pl.BlockSpec((1, n, tq), lambda b, i: (b, 0, jnp.maximum(i - first, 0)))
    return pl.pallas_call(
        _proj_c_prompt_kernel,
        grid=(B, S // tq),
        in_specs=in_specs,
        out_specs=[pl.BlockSpec((1, tq, n), lambda b, i: (b, i, 0)),
                   pl.BlockSpec((1, 2 * n, tq), lambda b, i: (b, 0, i)),
                   pl.BlockSpec((1, 2 * n, tq), lambda b, i: (b, 0, i)),
                   win_spec, win_spec],
        out_shape=[jax.ShapeDtypeStruct((B, S, n), BF16),
                   jax.ShapeDtypeStruct((B, 2 * n, S), BF16),
                   jax.ShapeDtypeStruct((B, 2 * n, S), BF16),
                   jax.ShapeDtypeStruct((B, n, win_blocks * tq), F32),
                   jax.ShapeDtypeStruct((B, n, win_blocks * tq), F32)],
        compiler_params=_params(("parallel", "arbitrary")),
        name="proj_c_prompt",
    )(x, *weights, cq, s1, s2, ct, st)


def _proj_c_sample_kernel(x_ref, g_ref, wt_ref, ct_ref, st_ref, qt_ref, kt_ref, vt_ref):
    h = _rms(x_ref[...], g_ref[...]).astype(BF16)
    zt = _dot_nt(wt_ref[...], h)
    n = C_HEADS * HEAD_DIM
    ct, st = ct_ref[...], st_ref[...]
    for hh in range(C_HEADS):
        sl = slice(hh * HEAD_DIM, (hh + 1) * HEAD_DIM)
        qt_ref[sl, :] = _rope_rows(zt[sl], ct, st) * C_SCALE
        kt_ref[sl, :] = _rope_rows(zt[n + hh * HEAD_DIM:n + (hh + 1) * HEAD_DIM], ct, st)
    vt_ref[...] = zt[2 * n:3 * n]


def _proj_c_sample(x, g, wt, ct, st):
    T, D = x.shape
    n = C_HEADS * HEAD_DIM
    vm = pl.BlockSpec(memory_space=pltpu.VMEM)
    return pl.pallas_call(
        _proj_c_sample_kernel,
        in_specs=[vm] * 5,
        out_specs=[vm] * 3,
        out_shape=[jax.ShapeDtypeStruct((n, T), F32)] * 3,
        compiler_params=pltpu.CompilerParams(vmem_limit_bytes=VMEM_LIMIT),
        name="proj_c_sample",
    )(x, g, wt, ct, st)


def _qlat_kernel(q_ref, w_ref, o_ref):
    for hh in range(MLA_HEADS):
        o_ref[:, hh * MLA_KV_RANK:(hh + 1) * MLA_KV_RANK] = _dot(
            q_ref[:, hh * LANES:(hh + 1) * LANES], w_ref[hh])


def _qlat(qm, wukp):
    T = qm.shape[0]
    vm = pl.BlockSpec(memory_space=pltpu.VMEM)
    return pl.pallas_call(
        _qlat_kernel, in_specs=[vm, vm], out_specs=vm,
        out_shape=jax.ShapeDtypeStruct((T, MLA_HEADS * MLA_KV_RANK), F32),
        name="mla_q_absorb",
    )(qm, wukp)


def _omla_kernel(o_ref, w_ref, y_ref):
    for p in range(MLA_HEADS // 2):
        acc = None
        for a in range(2):
            hh = 2 * p + a
            t = _dot(o_ref[:, hh * MLA_KV_RANK:(hh + 1) * MLA_KV_RANK].astype(BF16), w_ref[hh])
            acc = t if acc is None else acc + t
        y_ref[:, p * LANES:(p + 1) * LANES] = acc.astype(BF16)


def _omla(olat, wuvp):
    T = olat.shape[0]
    vm = pl.BlockSpec(memory_space=pltpu.VMEM)
    return pl.pallas_call(
        _omla_kernel, in_specs=[vm, vm], out_specs=vm,
        out_shape=jax.ShapeDtypeStruct((T, MLA_HEADS * HEAD_DIM), BF16),
        name="mla_v_expand",
    )(olat, wuvp)


def _decode_kernel(pt_ref, qbd_ref, ql_ref, qpe_ref, knew_ref, vnew_ref, cnew_ref, pnew_ref,
                   lfnew_ref, kt_hbm, vt_hbm, ckv_hbm, kpet_hbm, lft_hbm, of_ref, ol_ref,
                   kbuf, vbuf, cbuf, pbuf, lbuf, sem, *, n_pages, chunk):
    b = pl.program_id(0)
    nb = pl.num_programs(0)
    n_chunks = n_pages // chunk
    hbm = (kt_hbm, vt_hbm, ckv_hbm, kpet_hbm, lft_hbm)
    bufs = (kbuf, vbuf, cbuf, pbuf, lbuf)

    def copies(seq, c, slot):
        first = n_pages - (c + 1) * chunk
        out = []
        for pg in range(chunk):
            page = pt_ref[seq, first + pg]
            for kind in range(5):
                out.append(pltpu.make_async_copy(hbm[kind].at[page], bufs[kind].at[slot, pg],
                                                 sem.at[slot, kind]))
        return out

    def start(seq, c, slot):
        for cp in copies(seq, c, slot):
            cp.start()

    def wait(seq, c, slot):
        for cp in copies(seq, c, slot):
            cp.wait()

    @pl.when(b == 0)
    def _():
        start(0, 0, 0)

    qbd = qbd_ref[0].astype(BF16)
    ql = ql_ref[0].astype(BF16)
    qpe = qpe_ref[0]
    lane = lax.broadcasted_iota(jnp.int32, (8, LANES), 1)

    def process(c, slot, state):
        carry, mf, lf_, af, mm, lm, am = state

        @pl.when(c + 1 < n_chunks)
        def _():
            start(b, c + 1, 1 - slot)

        @pl.when(jnp.logical_and(c + 1 == n_chunks, b + 1 < nb))
        def _():
            start(b + 1, 0, 1 - slot)

        wait(b, c, slot)
        sf, sm, cps = [], [], []
        for pg in reversed(range(chunk)):
            x = lbuf[slot, pg]
            y = _lane_suffix_exclusive(x, lane)
            bias = (y + carry) * LOG2E
            carry = carry + (y[:, 0:1] + x[:, 0:1])
            sf.append(_dot(qbd, kbuf[slot, pg].astype(BF16)) + bias)
            cp = cbuf[slot, pg].astype(BF16)
            cps.append(cp)
            sm.append(_dot_nt(ql, cp) + _dot(qpe, pbuf[slot, pg].astype(BF16)))
        sf = jnp.concatenate(sf, axis=1)
        sm = jnp.concatenate(sm, axis=1)

        def soft(s, m, l):
            m_new = jnp.maximum(m, jnp.max(s, axis=1, keepdims=True))
            alpha = jnp.exp2(m - m_new)
            p = jnp.exp2(s - m_new)
            return m_new, alpha, alpha * l + jnp.sum(p, axis=1, keepdims=True), p.astype(BF16)

        mf, alf, lf_, pf = soft(sf, mf, lf_)
        mm, alm, lm, pm = soft(sm, mm, lm)
        af = af * alf
        am = am * alm
        for t, pg in enumerate(reversed(range(chunk))):
            af = af + _dot_nt(pf[:, t * LANES:(t + 1) * LANES], vbuf[slot, pg].astype(BF16))
            am = am + _dot(pm[:, t * LANES:(t + 1) * LANES], cps[t])
        return carry, mf, lf_, af, mm, lm, am

    neg = jnp.full((8, 1), NEG_INF, F32)
    zero1 = jnp.zeros((8, 1), F32)
    zacc = jnp.zeros((8, 2 * LANES), F32)
    state = (lfnew_ref[0], neg, zero1, zacc, neg, zero1, zacc)

    def pair(t, state):
        state = process(2 * t, 0, state)
        return process(2 * t + 1, 1, state)

    state = lax.fori_loop(0, n_chunks // 2, pair, state)
    _, mf, lf_, af, mm, lm, am = state

    qbd32, ql32 = qbd_ref[0], ql_ref[0]
    s_new = jnp.sum(qbd32 * knew_ref[0], axis=1, keepdims=True)
    m_fin = jnp.maximum(mf, s_new)
    al, p_new = jnp.exp2(mf - m_fin), jnp.exp2(s_new - m_fin)
    of_ref[0] = (af * al + p_new * vnew_ref[0]) / (lf_ * al + p_new)
    s_new = (jnp.sum(ql32 * cnew_ref[0], axis=1, keepdims=True)
             + jnp.sum(qpe.astype(F32) * pnew_ref[0], axis=1, keepdims=True))
    m_fin = jnp.maximum(mm, s_new)
    al, p_new = jnp.exp2(mm - m_fin), jnp.exp2(s_new - m_fin)
    ol_ref[0] = (am * al + p_new * cnew_ref[0]) / (lm * al + p_new)


def _decode(page_table, qbd, ql, qpe, knew, vnew, cnew, pnew, lfnew, kt_pages, vt_pages,
            ckv_pages, kpet_pages, lft_pages, *, chunk):
    nseq, n_pages = page_table.shape
    assert n_pages % (2 * chunk) == 0

    def seq_spec(shape):
        return pl.BlockSpec((1,) + shape, lambda b, pt: (b, 0, 0))

    anyspec = pl.BlockSpec(memory_space=pl.ANY)
    grid_spec = pltpu.PrefetchScalarGridSpec(
        num_scalar_prefetch=1,
        grid=(nseq,),
        in_specs=[seq_spec((8, 256)), seq_spec((8, 256)), seq_spec((8, MLA_ROPE)),
                  seq_spec((1, 256)), seq_spec((1, 256)), seq_spec((1, 256)),
                  seq_spec((1, MLA_ROPE)), seq_spec((8, 1))] + [anyspec] * 5,
        out_specs=[seq_spec((8, 256)), seq_spec((8, 256))],
        scratch_shapes=[pltpu.VMEM((2, chunk, 256, PAGE), F32),
                        pltpu.VMEM((2, chunk, 256, PAGE), F32),
                        pltpu.VMEM((2, chunk, PAGE, MLA_KV_RANK), F32),
                        pltpu.VMEM((2, chunk, MLA_ROPE, PAGE), F32),
                        pltpu.VMEM((2, chunk, FOX_HEADS, PAGE), F32),
                        pltpu.SemaphoreType.DMA((2, 5))],
    )
    kern = functools.partial(_decode_kernel, n_pages=n_pages, chunk=chunk)
    return pl.pallas_call(
        kern,
        grid_spec=grid_spec,
        out_shape=[jax.ShapeDtypeStruct((nseq, 8, 256), F32)] * 2,
        compiler_params=_params(("arbitrary",)),
        name="paged_decode",
    )(page_table, qbd, ql, qpe, knew, vnew, cnew, pnew, lfnew, kt_pages, vt_pages, ckv_pages,
      kpet_pages, lft_pages)


def _win_kernel(qt_ref, ktn_ref, vtn_ref, logc_ref, kin_ref, vin_ref, ot_ref, kout_ref, vout_ref,
                *, heads, log_self):
    b = pl.program_id(1)
    nseq = qt_ref.shape[1]
    sel = lax.broadcasted_iota(jnp.int32, (heads * HEAD_DIM, nseq), 1) == b

    def column(ref):
        return jnp.sum(jnp.where(sel, ref[...], 0.0), axis=1, keepdims=True)

    qcol, kcol, vcol = column(qt_ref), column(ktn_ref), column(vtn_ref)
    w = kin_ref.shape[3]
    lane = lax.broadcasted_iota(jnp.int32, (HEAD_DIM, w), 1)
    logc = logc_ref[...]
    cols = []
    for hh in range(heads):
        sl = slice(hh * HEAD_DIM, (hh + 1) * HEAD_DIM)
        q, kn, vn = qcol[sl], kcol[sl], vcol[sl]
        kin, vin = kin_ref[0, hh], vin_ref[0, hh]
        s = jnp.sum(kin * q, axis=0, keepdims=True) + logc
        s_new = jnp.sum(q * kn, axis=0, keepdims=True) + log_self
        m = jnp.maximum(jnp.max(s, axis=1, keepdims=True), s_new)
        p = jnp.exp(s - m)
        p_new = jnp.exp(s_new - m)
        l = jnp.sum(p, axis=1, keepdims=True) + p_new
        o = (jnp.sum(vin * p, axis=1, keepdims=True) + p_new * vn) / l
        cols.append(o)
        kout_ref[0, hh] = jnp.where(lane == w - 1, kn, pltpu.roll(kin, w - 1, 1))
        vout_ref[0, hh] = jnp.where(lane == w - 1, vn, pltpu.roll(vin, w - 1, 1))
    ocol = jnp.concatenate(cols, axis=0)

    @pl.when(b == 0)
    def _():
        ot_ref[...] = jnp.zeros_like(ot_ref)

    ot_ref[...] = jnp.where(sel, ocol, ot_ref[...])


def _win(qt, ktn, vtn, logc, kin, vin, *, heads, log_self):
    nseq, nh, d, w = kin.shape
    rows = heads * HEAD_DIM
    col_spec = pl.BlockSpec((rows, nseq), lambda hb, b: (hb, 0))
    buf_spec = pl.BlockSpec((1, heads, d, w), lambda hb, b: (b, hb, 0, 0))
    kern = functools.partial(_win_kernel, heads=heads, log_self=log_self)
    return pl.pallas_call(
        kern,
        grid=(nh // heads, nseq),
        in_specs=[col_spec, col_spec, col_spec, _const_spec(logc.shape), buf_spec, buf_spec],
        out_specs=[col_spec, buf_spec, buf_spec],
        out_shape=[jax.ShapeDtypeStruct((nh * d, nseq), F32),
                   jax.ShapeDtypeStruct(kin.shape, F32), jax.ShapeDtypeStruct(vin.shape, F32)],
        compiler_params=_params(("parallel", "arbitrary")),
        name="window_decode",
    )(qt, ktn, vtn, logc, kin, vin)


def _rope_tables_a(pos, scale):
    half = MLA_ROPE // 2
    inv = ROPE_THETA ** (-jnp.arange(half, dtype=F32) / half)
    ang = pos.astype(F32)[:, None] * inv[None, :]
    cos, sin = jnp.cos(ang), jnp.sin(ang)
    t = pos.shape[0]
    one, zero = jnp.ones((t, MLA_NOPE), F32), jnp.zeros((t, MLA_NOPE), F32)
    z16, z32 = jnp.zeros((t, half), F32), jnp.zeros((t, 32), F32)
    cq = jnp.concatenate([one, cos, cos, z32], axis=1) * scale
    s1 = jnp.concatenate([zero, -sin, z16, z32], axis=1) * scale
    s2 = jnp.concatenate([zero, z16, sin, z32], axis=1) * scale
    return cq, s1, s2, cos.T, sin.T


def _rope_tables_c(pos, scale):
    half = C_ROT // 2
    inv = ROPE_THETA ** (-jnp.arange(half, dtype=F32) / half)
    ang = pos.astype(F32)[:, None] * inv[None, :]
    cos, sin = jnp.cos(ang), jnp.sin(ang)
    t = pos.shape[0]
    one, zero = jnp.ones((t, HEAD_DIM - C_ROT), F32), jnp.zeros((t, HEAD_DIM - C_ROT), F32)
    z8 = jnp.zeros((t, half), F32)
    cq = jnp.concatenate([cos, cos, one] * 2, axis=1) * scale
    s1 = jnp.concatenate([-sin, z8, zero] * 2, axis=1) * scale
    s2 = jnp.concatenate([z8, sin, zero] * 2, axis=1) * scale
    return cq, s1, s2, cos.T, sin.T


def _band_table(tq, tk, nb_win):
    od = np.arange(-(tq // tk - 1), nb_win + 1)
    d = od[:, None, None] * tk + np.arange(tq)[None, :, None] - np.arange(tk)[None, None, :]
    cnt = np.zeros(d.shape, np.float64)
    for w, r in C_BRANCHES:
        cnt += (d >= 0) & (d <= w) & (d % r == 0)
    return jnp.asarray(np.where(cnt > 0, np.log2(np.maximum(cnt, 1.0)), NEG_INF), F32)


def _sample_logc(w):
    d = w - np.arange(w)
    cnt = np.zeros(d.shape, np.float64)
    for ww, r in C_BRANCHES:
        cnt += (d <= ww) & (d % r == 0)
    return jnp.asarray(np.where(cnt > 0, np.log(np.maximum(cnt, 1.0)), NEG_INF), F32)[None, :]


def _prep_a(w_in, fox_bf, gq, wqup, gkv, wuk, wuv, g_mix):
    o = np.cumsum((0, 512, 256, 256, 8, MLA_Q_RANK, MLA_KV_RANK, MLA_ROPE))
    wq, wk, wv, wf, wcq, wckv, wkpe = (w_in[:, o[t]:o[t + 1]] for t in range(7))
    wt = jnp.concatenate([wk, wv, wkpe, wf], axis=1).T.astype(BF16)
    up = wqup.reshape(MLA_Q_RANK, MLA_HEADS, MLA_NOPE + MLA_ROPE)
    up = jnp.pad(up, ((0, 0), (0, 0), (0, LANES - MLA_NOPE - MLA_ROPE)))
    wukt = jnp.transpose(wuk, (1, 2, 0)).reshape(MLA_HEADS * MLA_NOPE, MLA_KV_RANK)
    wuvt = jnp.transpose(wuv, (1, 2, 0)).reshape(MLA_HEADS * HEAD_DIM, MLA_KV_RANK)
    wukp = jnp.pad(jnp.transpose(wuk, (1, 2, 0)), ((0, 0), (0, LANES - MLA_NOPE), (0, 0)))
    wuv_h = jnp.transpose(wuv, (1, 0, 2))
    z = jnp.zeros_like(wuv_h)
    even = jnp.concatenate([wuv_h, z], axis=2)
    odd = jnp.concatenate([z, wuv_h], axis=2)
    wuvp = jnp.where((jnp.arange(MLA_HEADS) % 2 == 0)[:, None, None], even, odd)
    return dict(g=g_mix[None, :], wq=wq.astype(BF16), wcq=wcq.astype(BF16), wckv=wckv.astype(BF16),
                wt=wt, bf=fox_bf[:, None], gq=gq[None, :],
                wqup=up.reshape(MLA_Q_RANK, MLA_HEADS * LANES).astype(BF16), gkv=gkv[None, :],
                wukt=wukt.astype(BF16), wuvt=wuvt.astype(BF16), wukp=wukp.astype(BF16),
                wuvp=wuvp.astype(BF16))


def kernel(x_prompt, x_sample, cache_fox_k, cache_fox_v, cache_fox_logf, cache_mla_ckv, cache_mla_kpe, cache_win_k, cache_win_v, page_table, a_w_in, a_fox_bf, a_mla_gq, a_mla_wqup, a_mla_gkv, a_mla_wuk, a_mla_wuv, a_w_out, c_w_in, c_w_out, norm_mix, norm_ffn, ffn_w_up, ffn_w_down, norm_final):
    B, S, D = x_prompt.shape
    nseq = x_sample.shape[0]
    past_len = page_table.shape[1] * PAGE
    n_pool = cache_fox_k.shape[1]
    tq = PROJ_TILE
    ftq, ftk = FLASH_TQ, FLASH_TK
    pos_p = jnp.arange(S)
    pos_s = jnp.full((nseq,), past_len)

    wa = _prep_a(a_w_in[0], a_fox_bf[0], a_mla_gq[0], a_mla_wqup[0], a_mla_gkv[0], a_mla_wuk[0],
                 a_mla_wuv[0], norm_mix[0])
    wo_a = a_w_out[0].astype(BF16)
    gf0, wup0, wdn0 = norm_ffn[0][None, :], ffn_w_up[0].astype(BF16), ffn_w_down[0].astype(BF16)

    (qf, qm, kt, vt, lft, ckv, kpet, kfp, vf2, kmp, vm) = _proj_a(
        x_prompt, wa, _rope_tables_a(pos_p, MLA_SCALE * LOG2E), tq=tq)
    rct = _suffix(lft)
    o_fox = _flash(qf, kfp, vf2, rct.reshape(B * FOX_KV, 2, S), None, n_pairs=FOX_KV,
                   q_per_head=False, v_per_head=False, tq=ftq, tk=ftk)
    o_mla = _flash(qm, kmp, vm, None, None, n_pairs=MLA_HEADS // 2, q_per_head=True,
                   v_per_head=True, tq=ftq,
                   tk=ftk)
    xp = _out_mlp(x_prompt.reshape(B * S, D), [o_fox.reshape(B * S, -1), o_mla.reshape(B * S, -1)],
                  wo_a, gf0, wup0, wdn0, None, tm=512)

    p_fox_k = jnp.transpose(kt.reshape(B, FOX_KV, HEAD_DIM, S), (0, 3, 1, 2))[None]
    p_fox_v = jnp.transpose(vt.reshape(B, FOX_KV, HEAD_DIM, S), (0, 3, 1, 2))[None]
    p_fox_logf = jnp.transpose(lft, (0, 2, 1))[None]
    p_mla_ckv = ckv[None]
    p_mla_kpe = jnp.transpose(kpet, (0, 2, 1))[None]

    xs = x_sample.reshape(1, nseq, D)
    (qf_s, qm_s, kt_s, vt_s, lft_s, ckv_s, kpet_s, _, _, _, _) = _proj_a(
        xs, wa, _rope_tables_a(pos_s, MLA_SCALE * LOG2E), tq=nseq)
    qf_s = qf_s[0].astype(F32).reshape(nseq, FOX_KV, 2, HEAD_DIM)
    eye = jnp.eye(FOX_KV, dtype=F32)
    qbd = jnp.einsum("bkgd,kj->bkgjd", qf_s, eye).reshape(nseq, FOX_HEADS, FOX_KV * HEAD_DIM)
    ql = _qlat(qm_s[0], wa["wukp"]).reshape(nseq, MLA_HEADS, MLA_KV_RANK)
    qpe = qm_s[0].reshape(nseq, MLA_HEADS, LANES)[:, :, MLA_NOPE:MLA_NOPE + MLA_ROPE]
    knew = kt_s[0].T[:, None, :]
    vnew = vt_s[0].T[:, None, :]
    cnew = ckv_s[0][:, None, :]
    pnew = kpet_s[0].T[:, None, :]
    lfnew = lft_s[0].T[:, :, None]

    kt_pages = jnp.transpose(cache_fox_k[0], (0, 2, 3, 1)).reshape(n_pool, 256, PAGE)
    vt_pages = jnp.transpose(cache_fox_v[0], (0, 2, 3, 1)).reshape(n_pool, 256, PAGE)
    lft_pages = jnp.transpose(cache_fox_logf[0], (0, 2, 1))
    kpet_pages = jnp.transpose(cache_mla_kpe[0], (0, 2, 1))
    of_full, o_lat = _decode(page_table, qbd, ql, qpe, knew, vnew, cnew, pnew, lfnew, kt_pages,
                             vt_pages, cache_mla_ckv[0], kpet_pages, lft_pages, chunk=DECODE_CHUNK)
    of_full = of_full.reshape(nseq, FOX_KV, 2, FOX_KV, HEAD_DIM)
    o_fox_s = jnp.einsum("bkgjd,kj->bkgd", of_full, eye).reshape(nseq, FOX_HEADS * HEAD_DIM)
    o_mla_s = _omla(o_lat.reshape(nseq, -1), wa["wuvp"])
    xs2 = _out_mlp(x_sample.reshape(nseq, D), [o_fox_s, o_mla_s], wo_a, gf0, wup0, wdn0, None,
                   tm=nseq)

    s_fox_k = kt_s[0].T.reshape(1, nseq, 1, FOX_KV, HEAD_DIM)
    s_fox_v = vt_s[0].T.reshape(1, nseq, 1, FOX_KV, HEAD_DIM)
    s_fox_logf = lft_s[0].T.reshape(1, nseq, 1, FOX_HEADS)
    s_mla_ckv = ckv_s.reshape(1, nseq, 1, MLA_KV_RANK)
    s_mla_kpe = kpet_s[0].T.reshape(1, nseq, 1, MLA_ROPE)

    n = C_HEADS * HEAD_DIM
    wc = c_w_in[0]
    wo_c = c_w_out[0].astype(BF16)
    g1 = norm_mix[1][None, :]
    gf1, wup1, wdn1 = norm_ffn[1][None, :], ffn_w_up[1].astype(BF16), ffn_w_down[1].astype(BF16)
    gfin = norm_final[None, :]
    nb_win = C_WIN // ftk
    qc, kcp, vc, ktf, vtf = _proj_c_prompt(
        xp.reshape(B, S, D), g1, wc[:, :n].astype(BF16), wc[:, n:].T.astype(BF16),
        _rope_tables_c(pos_p, C_SCALE * LOG2E), tq=tq)
    o_c = _flash(qc, kcp, vc, None, _band_table(ftq, ftk, nb_win), n_pairs=C_HEADS // 2,
                 q_per_head=False, v_per_head=True, tq=ftq, tk=ftk, nb_win=nb_win)
    y_prompt = _out_mlp(xp, [o_c.reshape(B * S, n)], wo_c, gf1, wup1, wdn1, gfin, tm=512)
    y_prompt = y_prompt.reshape(B, S, D)
    keep = ktf.shape[2]
    p_win_k = jnp.transpose(ktf.reshape(B, C_HEADS, HEAD_DIM, keep), (0, 3, 1, 2))[None]
    p_win_v = jnp.transpose(vtf.reshape(B, C_HEADS, HEAD_DIM, keep), (0, 3, 1, 2))[None]

    _, _, _, ct_s, st_s = _rope_tables_c(pos_s, 1.0)
    qt_s, ktn_s, vtn_s = _proj_c_sample(xs2, g1, wc.T.astype(BF16), ct_s, st_s)
    kin = jnp.transpose(cache_win_k[0], (0, 2, 3, 1))
    vin = jnp.transpose(cache_win_v[0], (0, 2, 3, 1))
    w = kin.shape[3]
    ot_s, kout, vout = _win(qt_s, ktn_s, vtn_s, _sample_logc(w), kin, vin, heads=WIN_HEADS,
                            log_self=math.log(len(C_BRANCHES)))
    y_sample = _out_mlp(xs2, [ot_s.T], wo_c, gf1, wup1, wdn1, gfin, tm=nseq)
    y_sample = y_sample.reshape(nseq, 1, D)
    s_win_k = jnp.transpose(kout, (0, 3, 1, 2))[None]
    s_win_v = jnp.transpose(vout, (0, 3, 1, 2))[None]

    return (y_prompt, y_sample, p_fox_k, p_fox_v, p_fox_logf, p_mla_ckv, p_mla_kpe, p_win_k, p_win_v,
            s_fox_k, s_fox_v, s_fox_logf, s_mla_ckv, s_mla_kpe, s_win_k, s_win_v)
```

```python
import functools
import math

import numpy as np
import jax
import jax.numpy as jnp
from jax import lax
from jax.experimental import pallas as pl
from jax.experimental.pallas import tpu as pltpu

F32 = jnp.float32
BF16 = jnp.bfloat16
EPS = 1e-6
NEG_INF = -1e30
ROPE_THETA = 500000.0

LANES = 128
HEAD_DIM = 64
PAGE = 128

FOX_HEADS = 8
FOX_KV = 4
FOX_SCALE = HEAD_DIM ** -0.5
MLA_HEADS = 8
MLA_Q_RANK = 384
MLA_KV_RANK = 256
MLA_NOPE = 64
MLA_ROPE = 32
MLA_SCALE = (MLA_NOPE + MLA_ROPE) ** -0.5
C_HEADS = 16
C_ROT = 16
C_SCALE = HEAD_DIM ** -0.5
LOG2E = math.log2(math.e)
C_BRANCHES = ((128, 1), (512, 4), (2048, 16))
C_WIN = 2048

VMEM_LIMIT = 56 * 1024 * 1024

PROJ_TILE = 512
FLASH_TQ = 1024
FLASH_TK = 1024
DECODE_CHUNK = 16
DECODE_STREAMS = 1
WIN_HEADS = 8


def _rms(x, g):
    return (x * lax.rsqrt(jnp.mean(x * x, axis=-1, keepdims=True) + EPS)) * g


def _dot(a, b):
    return jnp.dot(a, b, preferred_element_type=F32)


def _dot_nt(a, b):
    return lax.dot_general(a, b, (((1,), (1,)), ((), ())), preferred_element_type=F32)


def _const_spec(shape):
    n = len(shape)
    return pl.BlockSpec(shape, lambda *_: (0,) * n, pipeline_mode=pl.Buffered(1))


def _params(sem, limit=VMEM_LIMIT):
    return pltpu.CompilerParams(dimension_semantics=sem, vmem_limit_bytes=limit)


def _store_value_slab(ref, base, v):
    t = v.shape[1]
    ones = jnp.where(lax.broadcasted_iota(jnp.int32, (16, t), 0) == 0, 1.0, 0.0).astype(BF16)
    ref[0, base:base + HEAD_DIM, :] = v
    ref[0, base + HEAD_DIM:base + HEAD_DIM + 16, :] = ones
    ref[0, base + HEAD_DIM + 16:base + LANES, :] = jnp.zeros((LANES - HEAD_DIM - 16, t), BF16)


def _log_sigmoid(x):
    return jnp.minimum(x, 0.0) - jnp.log1p(jnp.exp(-jnp.abs(x)))


def _proj_a_kernel(x_ref, g_ref, wq_ref, wcq_ref, wckv_ref, wt_ref, bf_ref, gq_ref, wqup_ref,
                   gkv_ref, wukt_ref, wuvt_ref, cq_ref, s1_ref, s2_ref, ct_ref, st_ref,
                   qf_ref, qm_ref, kt_ref, vt_ref, lft_ref, ckv_ref, kpet_ref,
                   kfp_ref, vf2_ref, kmp_ref, vm_ref):
    h = _rms(x_ref[0], g_ref[...]).astype(BF16)
    qf_ref[0] = (_dot(h, wq_ref[...]) * (FOX_SCALE * LOG2E)).astype(BF16)

    cqn = _rms(_dot(h, wcq_ref[...]), gq_ref[...]).astype(BF16)
    qm = _dot(cqn, wqup_ref[...])
    cq, s1, s2 = cq_ref[...], s1_ref[...], s2_ref[...]
    for hh in range(MLA_HEADS):
        blk = qm[:, hh * LANES:(hh + 1) * LANES]
        rot = blk * cq + pltpu.roll(blk, LANES - 16, 1) * s1 + pltpu.roll(blk, 16, 1) * s2
        qm_ref[0, :, hh * LANES:(hh + 1) * LANES] = rot.astype(BF16)

    ckvn = _rms(_dot(h, wckv_ref[...]), gkv_ref[...])
    ckv_ref[0] = ckvn
    cb = ckvn.astype(BF16)
    knt = _dot_nt(wukt_ref[...], cb)
    vmt = _dot_nt(wuvt_ref[...], cb).astype(BF16)
    for hh in range(MLA_HEADS):
        _store_value_slab(vm_ref, hh * LANES, vmt[hh * HEAD_DIM:(hh + 1) * HEAD_DIM])

    zt = _dot_nt(wt_ref[...], h)
    kt = zt[0:256]
    vt = zt[256:512]
    kt_ref[0] = kt
    vt_ref[0] = vt
    x1, x2 = zt[512:528], zt[528:544]
    ct, st = ct_ref[...], st_ref[...]
    kpet = jnp.concatenate([x1 * ct - x2 * st, x2 * ct + x1 * st], axis=0)
    kpet_ref[0] = kpet
    lft_ref[0] = _log_sigmoid(zt[544:552] + bf_ref[...])

    tq = kt.shape[1]
    zero64 = jnp.zeros((HEAD_DIM, tq), BF16)
    ktb, vtb, kpb = kt.astype(BF16), vt.astype(BF16), kpet.astype(BF16)
    for hh in range(FOX_HEADS):
        g, a = hh // 2, hh % 2
        kg = ktb[g * HEAD_DIM:(g + 1) * HEAD_DIM]
        base = hh * LANES
        kfp_ref[0, base + a * HEAD_DIM:base + (a + 1) * HEAD_DIM, :] = kg
        kfp_ref[0, base + (1 - a) * HEAD_DIM:base + (2 - a) * HEAD_DIM, :] = zero64
    for g in range(FOX_KV):
        _store_value_slab(vf2_ref, g * LANES, vtb[g * HEAD_DIM:(g + 1) * HEAD_DIM])
    zero32 = jnp.zeros((32, tq), BF16)
    kntb = knt.astype(BF16)
    for hh in range(MLA_HEADS):
        base = hh * LANES
        kmp_ref[0, base:base + MLA_NOPE, :] = kntb[hh * MLA_NOPE:(hh + 1) * MLA_NOPE]
        kmp_ref[0, base + MLA_NOPE:base + MLA_NOPE + MLA_ROPE, :] = kpb
        kmp_ref[0, base + MLA_NOPE + MLA_ROPE:base + LANES, :] = zero32


def _proj_a(x, wa, tabs, *, tq):
    B, S, D = x.shape
    cq, s1, s2, ct, st = tabs
    weights = (wa["g"], wa["wq"], wa["wcq"], wa["wckv"], wa["wt"], wa["bf"], wa["gq"], wa["wqup"],
               wa["gkv"], wa["wukt"], wa["wuvt"])
    in_specs = [pl.BlockSpec((1, tq, D), lambda b, i: (b, i, 0))]
    in_specs += [_const_spec(w.shape) for w in weights]
    in_specs += [pl.BlockSpec((tq, LANES), lambda b, i: (i, 0))] * 3
    in_specs += [pl.BlockSpec((16, tq), lambda b, i: (0, i))] * 2

    def nat(n, dt):
        return jax.ShapeDtypeStruct((B, S, n), dt), pl.BlockSpec((1, tq, n), lambda b, i: (b, i, 0))

    def trn(n, dt):
        return jax.ShapeDtypeStruct((B, n, S), dt), pl.BlockSpec((1, n, tq), lambda b, i: (b, 0, i))

    outs = [nat(512, BF16), nat(1024, BF16), trn(256, F32), trn(256, F32), trn(8, F32),
            nat(256, F32), trn(32, F32), trn(1024, BF16), trn(512, BF16), trn(1024, BF16),
            trn(1024, BF16)]
    return pl.pallas_call(
        _proj_a_kernel,
        grid=(B, S // tq),
        in_specs=in_specs,
        out_specs=[o[1] for o in outs],
        out_shape=[o[0] for o in outs],
        compiler_params=_params(("parallel", "parallel")),
        name="proj_a",
    )(x, *weights, cq, s1, s2, ct, st)


def _lane_suffix_exclusive(x, lane):
    y = jnp.where(lane < LANES - 1, pltpu.roll(x, LANES - 1, 1), 0.0)
    for k in (1, 2, 4, 8, 16, 32, 64):
        y = y + jnp.where(lane < LANES - k, pltpu.roll(y, LANES - k, 1), 0.0)
    return y


def _suffix_kernel(lf_ref, rc_ref):
    x = lf_ref[0]
    s = x.shape[1]
    nblk = s // LANES
    lane = lax.broadcasted_iota(jnp.int32, x.shape, 1)
    inb = lane & (LANES - 1)
    steps = (1, 2, 4, 8, 16, 32, 64)
    y = jnp.where(inb < LANES - 1, pltpu.roll(x, s - 1, 1), 0.0)
    for k in steps:
        y = y + jnp.where(inb < LANES - k, pltpu.roll(y, s - k, 1), 0.0)
    tot = jnp.where(inb == 0, y + x, 0.0)
    for k in steps:
        tot = tot + jnp.where(inb >= k, pltpu.roll(tot, k, 1), 0.0)
    later = jnp.where(lane < s - LANES, pltpu.roll(tot, s - LANES, 1), 0.0)
    k = 1
    while k < nblk:
        later = later + jnp.where(lane < s - k * LANES, pltpu.roll(later, s - k * LANES, 1), 0.0)
        k *= 2
    rc_ref[0] = (y + later) * LOG2E


def _suffix(lft):
    B, H, S = lft.shape
    return pl.pallas_call(
        _suffix_kernel,
        grid=(B,),
        in_specs=[pl.BlockSpec((1, H, S), lambda b: (b, 0, 0))],
        out_specs=pl.BlockSpec((1, H, S), lambda b: (b, 0, 0)),
        out_shape=jax.ShapeDtypeStruct((B, H, S), F32),
        compiler_params=_params(("parallel",)),
        name="suffix_logf",
    )(lft)


def _flash_kernel(*refs, tq, tk, nb_win, has_bias, has_tab):
    q0_ref, q1_ref, k0_ref, k1_ref, v0_ref, v1_ref = refs[:6]
    rest = list(refs[6:])
    b_ref = rest.pop(0) if has_bias else None
    tab_ref = rest.pop(0) if has_tab else None
    o_ref = rest.pop(0)
    i = pl.program_id(2)
    r = tq // tk
    diff = (lax.broadcasted_iota(jnp.int32, (tq, tk), 1)
            - lax.broadcasted_iota(jnp.int32, (tq, tk), 0))
    qs = (q0_ref[0], q1_ref[0])
    k_refs = (k0_ref, k1_ref)
    v_refs = (v0_ref, v1_ref)

    def block(od, carry):
        off = pl.multiple_of((i * r - od) * tk, tk)
        out = []
        for a in range(2):
            m, acc = carry[a]
            s = _dot(qs[a], k_refs[a][0, :, pl.ds(off, tk)])
            if has_bias:
                s = s + b_ref[0, a:a + 1, pl.ds(off, tk)]
            if has_tab:
                s = s + tab_ref[od + (r - 1)]
            elif isinstance(od, int):
                s = jnp.where(diff <= od * tk, s, NEG_INF)
            m_new = jnp.maximum(m, jnp.max(s, axis=1, keepdims=True))
            alpha = jnp.exp2(m - m_new)
            p = jnp.exp2((s - m_new).astype(BF16))
            acc = alpha * acc + _dot_nt(p, v_refs[a][0, :, pl.ds(off, tk)])
            out.append((m_new, acc))
        return tuple(out)

    init = (jnp.full((tq, 1), NEG_INF, F32), jnp.zeros((tq, LANES), F32))
    carry = (init, init)
    for od in range(0, -r, -1):
        carry = block(od, carry)
    n_prev = jnp.minimum(i * r, nb_win) if has_tab else i * r
    carry = lax.fori_loop(0, n_prev, lambda t, c: block(t + 1, c), carry)
    outs = [acc / acc[:, HEAD_DIM:HEAD_DIM + 1] for (_, acc) in carry]
    lane = lax.broadcasted_iota(jnp.int32, (tq, LANES), 1)
    o_ref[0] = jnp.where(lane < HEAD_DIM, outs[0],
                         pltpu.roll(outs[1], HEAD_DIM, 1)).astype(o_ref.dtype)


def _flash(q, kp, v, bias, tab, *, n_pairs, q_per_head, v_per_head, tq, tk, nb_win=0):
    B, S, _ = q.shape
    per_head = [lambda b, p, i: (b, i, 2 * p), lambda b, p, i: (b, i, 2 * p + 1)]
    qmaps = per_head if q_per_head else [lambda b, p, i: (b, i, p)] * 2
    if v_per_head:
        vmaps = [lambda b, p, i: (b, 2 * p, 0), lambda b, p, i: (b, 2 * p + 1, 0)]
    else:
        vmaps = [lambda b, p, i: (b, p, 0)] * 2
    in_specs = [pl.BlockSpec((1, tq, LANES), qmaps[0]), pl.BlockSpec((1, tq, LANES), qmaps[1]),
                pl.BlockSpec((1, LANES, S), lambda b, p, i: (b, 2 * p, 0)),
                pl.BlockSpec((1, LANES, S), lambda b, p, i: (b, 2 * p + 1, 0)),
                pl.BlockSpec((1, LANES, S), vmaps[0]), pl.BlockSpec((1, LANES, S), vmaps[1])]
    args = [q, q, kp, kp, v, v]
    if bias is not None:
        in_specs.append(pl.BlockSpec((1, 2, S), lambda b, p, i: (b * n_pairs + p, 0, 0)))
        args.append(bias)
    if tab is not None:
        in_specs.append(_const_spec(tab.shape))
        args.append(tab)
    kern = functools.partial(_flash_kernel, tq=tq, tk=tk, nb_win=nb_win,
                             has_bias=bias is not None, has_tab=tab is not None)
    return pl.pallas_call(
        kern,
        grid=(B, n_pairs, S // tq),
        in_specs=in_specs,
        out_specs=pl.BlockSpec((1, tq, LANES), lambda b, p, i: (b, i, p)),
        out_shape=jax.ShapeDtypeStruct((B, S, n_pairs * LANES), BF16),
        compiler_params=_params(("parallel", "parallel", "arbitrary")),
        name="flash",
    )(*args)


def _out_mlp_kernel(*refs, n_parts, f_chunk, final):
    x_ref = refs[0]
    o_refs = refs[1:1 + n_parts]
    wo_ref, gf_ref, wup_ref, wdn_ref = refs[1 + n_parts:5 + n_parts]
    rest = list(refs[5 + n_parts:])
    gfin_ref = rest.pop(0) if final else None
    y_ref = rest.pop(0)
    proj = None
    r0 = 0
    for o_ref in o_refs:
        n = o_ref.shape[1]
        t = _dot(o_ref[...].astype(BF16), wo_ref[r0:r0 + n, :])
        proj = t if proj is None else proj + t
        r0 += n
    y1 = x_ref[...] + proj
    hn = _rms(y1, gf_ref[...]).astype(BF16)
    acc = y1
    d_ff = wup_ref.shape[1]
    for c in range(d_ff // f_chunk):
        u = _dot(hn, wup_ref[:, c * f_chunk:(c + 1) * f_chunk])
        a = jnp.square(jnp.maximum(u, 0.0)).astype(BF16)
        acc = acc + _dot(a, wdn_ref[c * f_chunk:(c + 1) * f_chunk, :])
    if final:
        acc = _rms(acc, gfin_ref[...])
    y_ref[...] = acc


def _out_mlp(x, o_parts, wo, gf, wup, wdn, gfin, *, tm):
    T, D = x.shape
    final = gfin is not None
    in_specs = [pl.BlockSpec((tm, D), lambda i: (i, 0))]
    in_specs += [pl.BlockSpec((tm, o.shape[1]), lambda i: (i, 0)) for o in o_parts]
    weights = [wo, gf, wup, wdn] + ([gfin] if final else [])
    in_specs += [_const_spec(w.shape) for w in weights]
    kern = functools.partial(_out_mlp_kernel, n_parts=len(o_parts), f_chunk=1024, final=final)
    return pl.pallas_call(
        kern,
        grid=(T // tm,),
        in_specs=in_specs,
        out_specs=pl.BlockSpec((tm, D), lambda i: (i, 0)),
        out_shape=jax.ShapeDtypeStruct((T, D), F32),
        compiler_params=_params(("parallel",)),
        name="out_mlp",
    )(x, *o_parts, *weights)


def _rope_rows(z, ct, st):
    x1, x2 = z[0:8], z[8:16]
    return jnp.concatenate([x1 * ct - x2 * st, x2 * ct + x1 * st, z[16:]], axis=0)


def _proj_c_prompt_kernel(x_ref, g_ref, wq_ref, wkvt_ref, cq_ref, s1_ref, s2_ref, ct_ref, st_ref,
                          q_ref, kp_ref, v_ref, ktf_ref, vtf_ref):
    h = _rms(x_ref[0], g_ref[...]).astype(BF16)
    zq = _dot(h, wq_ref[...])
    cq, s1, s2 = cq_ref[...], s1_ref[...], s2_ref[...]
    for cb in range(C_HEADS // 2):
        blk = zq[:, cb * LANES:(cb + 1) * LANES]
        rot = blk * cq + pltpu.roll(blk, LANES - 8, 1) * s1 + pltpu.roll(blk, 8, 1) * s2
        q_ref[0, :, cb * LANES:(cb + 1) * LANES] = rot.astype(BF16)
    zt = _dot_nt(wkvt_ref[...], h)
    n = C_HEADS * HEAD_DIM
    ct, st = ct_ref[...], st_ref[...]
    tq = zt.shape[1]
    zero64 = jnp.zeros((HEAD_DIM, tq), BF16)
    for hh in range(C_HEADS):
        kh = _rope_rows(zt[hh * HEAD_DIM:(hh + 1) * HEAD_DIM], ct, st)
        ktf_ref[0, hh * HEAD_DIM:(hh + 1) * HEAD_DIM, :] = kh
        a = hh % 2
        base = hh * LANES
        kp_ref[0, base + a * HEAD_DIM:base + (a + 1) * HEAD_DIM, :] = kh.astype(BF16)
        kp_ref[0, base + (1 - a) * HEAD_DIM:base + (2 - a) * HEAD_DIM, :] = zero64
    vt = zt[n:2 * n]
    vtf_ref[0] = vt
    vtb = vt.astype(BF16)
    for hh in range(C_HEADS):
        _store_value_slab(v_ref, hh * LANES, vtb[hh * HEAD_DIM:(hh + 1) * HEAD_DIM])


def _proj_c_prompt(x, g, wq, wkvt, tabs, *, tq):
    B, S, D = x.shape
    cq, s1, s2, ct, st = tabs
    n = C_HEADS * HEAD_DIM
    win_blocks = min(C_WIN, S) // tq
    first = S // tq - win_blocks
    weights = (g, wq, wkvt)
    in_specs = [pl.BlockSpec((1, tq, D), lambda b, i: (b, i, 0))]
    in_specs += [_const_spec(w.shape) for w in weights]
    in_specs += [pl.BlockSpec((tq, LANES), lambda b, i: (i, 0))] * 3
    in_specs += [pl.BlockSpec((8, tq), lambda b, i: (0, i))] * 2
    win_spec = pl.BlockSpec((1, n, tq), lambda b, i: (b, 0, jnp.maximum(i - first, 0)))
    return pl.pallas_call(
        _proj_c_prompt_kernel,
        grid=(B, S // tq),
        in_specs=in_specs,
        out_specs=[pl.BlockSpec((1, tq, n), lambda b, i: (b, i, 0)),
                   pl.BlockSpec((1, 2 * n, tq), lambda b, i: (b, 0, i)),
                   pl.BlockSpec((1, 2 * n, tq), lambda b, i: (b, 0, i)),
                   win_spec, win_spec],
        out_shape=[jax.ShapeDtypeStruct((B, S, n), BF16),
                   jax.ShapeDtypeStruct((B, 2 * n, S), BF16),
                   jax.ShapeDtypeStruct((B, 2 * n, S), BF16),
                   jax.ShapeDtypeStruct((B, n, win_blocks * tq), F32),
                   jax.ShapeDtypeStruct((B, n, win_blocks * tq), F32)],
        compiler_params=_params(("parallel", "arbitrary")),
        name="proj_c_prompt",
    )(x, *weights, cq, s1, s2, ct, st)


def _proj_c_sample_kernel(x_ref, g_ref, wt_ref, ct_ref, st_ref, qt_ref, kt_ref, vt_ref):
    h = _rms(x_ref[...], g_ref[...]).astype(BF16)
    zt = _dot_nt(wt_ref[...], h)
    n = C_HEADS * HEAD_DIM
    ct, st = ct_ref[...], st_ref[...]
    for hh in range(C_HEADS):
        sl = slice(hh * HEAD_DIM, (hh + 1) * HEAD_DIM)
        qt_ref[sl, :] = _rope_rows(zt[sl], ct, st) * C_SCALE
        kt_ref[sl, :] = _rope_rows(zt[n + hh * HEAD_DIM:n + (hh + 1) * HEAD_DIM], ct, st)
    vt_ref[...] = zt[2 * n:3 * n]


def _proj_c_sample(x, g, wt, ct, st):
    T, D = x.shape
    n = C_HEADS * HEAD_DIM
    vm = pl.BlockSpec(memory_space=pltpu.VMEM)
    return pl.pallas_call(
        _proj_c_sample_kernel,
        in_specs=[vm] * 5,
        out_specs=[vm] * 3,
        out_shape=[jax.ShapeDtypeStruct((n, T), F32)] * 3,
        compiler_params=pltpu.CompilerParams(vmem_limit_bytes=VMEM_LIMIT),
        name="proj_c_sample",
    )(x, g, wt, ct, st)


def _qlat_kernel(q_ref, w_ref, o_ref):
    for hh in range(MLA_HEADS):
        o_ref[:, hh * MLA_KV_RANK:(hh + 1) * MLA_KV_RANK] = _dot(
            q_ref[:, hh * LANES:(hh + 1) * LANES], w_ref[hh])


def _qlat(qm, wukp):
    T = qm.shape[0]
    vm = pl.BlockSpec(memory_space=pltpu.VMEM)
    return pl.pallas_call(
        _qlat_kernel, in_specs=[vm, vm], out_specs=vm,
        out_shape=jax.ShapeDtypeStruct((T, MLA_HEADS * MLA_KV_RANK), F32),
        name="mla_q_absorb",
    )(qm, wukp)


def _omla_kernel(o_ref, w_ref, y_ref):
    for p in range(MLA_HEADS // 2):
        acc = None
        for a in range(2):
            hh = 2 * p + a
            t = _dot(o_ref[:, hh * MLA_KV_RANK:(hh + 1) * MLA_KV_RANK].astype(BF16), w_ref[hh])
            acc = t if acc is None else acc + t
        y_ref[:, p * LANES:(p + 1) * LANES] = acc.astype(BF16)


def _omla(olat, wuvp):
    T = olat.shape[0]
    vm = pl.BlockSpec(memory_space=pltpu.VMEM)
    return pl.pallas_call(
        _omla_kernel, in_specs=[vm, vm], out_specs=vm,
        out_shape=jax.ShapeDtypeStruct((T, MLA_HEADS * HEAD_DIM), BF16),
        name="mla_v_expand",
    )(olat, wuvp)


def _decode_kernel(pt_ref, qbd_ref, ql_ref, qpe_ref, knew_ref, vnew_ref, cnew_ref, pnew_ref,
                   lfnew_ref, kt_hbm, vt_hbm, ckv_hbm, kpet_hbm, lft_hbm, of_ref, ol_ref,
                   kbuf, vbuf, cbuf, pbuf, lbuf, sem, *, n_pages, chunk):
    b = pl.program_id(0)
    nb = pl.num_programs(0)
    n_chunks = n_pages // chunk
    hbm = (kt_hbm, vt_hbm, ckv_hbm, kpet_hbm, lft_hbm)
    bufs = (kbuf, vbuf, cbuf, pbuf, lbuf)

    def copies(seq, c, slot):
        first = n_pages - (c + 1) * chunk
        out = []
        for pg in range(chunk):
            page = pt_ref[seq, first + pg]
            for kind in range(5):
                out.append(pltpu.make_async_copy(hbm[kind].at[page], bufs[kind].at[slot, pg],
                                                 sem.at[slot, kind]))
        return out

    def start(seq, c, slot):
        for cp in copies(seq, c, slot):
            cp.start()

    def wait(seq, c, slot):
        for cp in copies(seq, c, slot):
            cp.wait()

    @pl.when(b == 0)
    def _():
        start(0, 0, 0)

    qbd = qbd_ref[0].astype(BF16)
    ql = ql_ref[0].astype(BF16)
    qpe = qpe_ref[0]
    lane = lax.broadcasted_iota(jnp.int32, (8, LANES), 1)

    def soft(s, st):
        m, l, _ = st
        m_new = jnp.maximum(m, jnp.max(s, axis=1, keepdims=True))
        alpha = jnp.exp2(m - m_new)
        p = jnp.exp2(s - m_new)
        return (m_new, alpha * l + jnp.sum(p, axis=1, keepdims=True), alpha), p.astype(BF16)

    def process(c, slot, state):
        carry, fox, mla = state

        @pl.when(c + 1 < n_chunks)
        def _():
            start(b, c + 1, 1 - slot)

        @pl.when(jnp.logical_and(c + 1 == n_chunks, b + 1 < nb))
        def _():
            start(b + 1, 0, 1 - slot)

        wait(b, c, slot)
        pages = list(reversed(range(chunk)))
        per = chunk // DECODE_STREAMS
        fox, mla = list(fox), list(mla)
        for g in range(DECODE_STREAMS):
            sf, sm, cps = [], [], []
            group = pages[g * per:(g + 1) * per]
            for pg in group:
                x = lbuf[slot, pg]
                y = _lane_suffix_exclusive(x, lane)
                bias = (y + carry) * LOG2E
                carry = carry + (y[:, 0:1] + x[:, 0:1])
                sf.append(_dot(qbd, kbuf[slot, pg].astype(BF16)) + bias)
                cp = cbuf[slot, pg].astype(BF16)
                cps.append(cp)
                sm.append(_dot_nt(ql, cp) + _dot(qpe, pbuf[slot, pg].astype(BF16)))
            (mf, lf_, alf), pf = soft(jnp.concatenate(sf, axis=1), fox[g][0])
            (mm, lm, alm), pm = soft(jnp.concatenate(sm, axis=1), mla[g][0])
            af = fox[g][1] * alf
            am = mla[g][1] * alm
            for t, pg in enumerate(group):
                af = af + _dot_nt(pf[:, t * LANES:(t + 1) * LANES], vbuf[slot, pg].astype(BF16))
                am = am + _dot(pm[:, t * LANES:(t + 1) * LANES], cps[t])
            fox[g] = ((mf, lf_, alf), af)
            mla[g] = ((mm, lm, alm), am)
        return carry, tuple(fox), tuple(mla)

    neg = jnp.full((8, 1), NEG_INF, F32)
    zero1 = jnp.zeros((8, 1), F32)
    zacc = jnp.zeros((8, 2 * LANES), F32)
    empty = tuple(((neg, zero1, zero1), zacc) for _ in range(DECODE_STREAMS))
    state = (lfnew_ref[0], empty, empty)

    def pair(t, state):
        state = process(2 * t, 0, state)
        return process(2 * t + 1, 1, state)

    _, fox, mla = lax.fori_loop(0, n_chunks // 2, pair, state)

    def finish(streams, s_new, v_new):
        m_fin = s_new
        for (m, _, _), _ in streams:
            m_fin = jnp.maximum(m_fin, m)
        p_new = jnp.exp2(s_new - m_fin)
        num, den = p_new * v_new, p_new
        for (m, l, _), acc in streams:
            al = jnp.exp2(m - m_fin)
            num, den = num + acc * al, den + l * al
        return num / den

    qbd32, ql32 = qbd_ref[0], ql_ref[0]
    s_new = jnp.sum(qbd32 * knew_ref[0], axis=1, keepdims=True)
    of_ref[0] = finish(fox, s_new, vnew_ref[0])
    s_new = (jnp.sum(ql32 * cnew_ref[0], axis=1, keepdims=True)
             + jnp.sum(qpe.astype(F32) * pnew_ref[0], axis=1, keepdims=True))
    ol_ref[0] = finish(mla, s_new, cnew_ref[0])


def _decode(page_table, qbd, ql, qpe, knew, vnew, cnew, pnew, lfnew, kt_pages, vt_pages,
            ckv_pages, kpet_pages, lft_pages, *, chunk):
    nseq, n_pages = page_table.shape
    assert n_pages % (2 * chunk) == 0

    def seq_spec(shape):
        return pl.BlockSpec((1,) + shape, lambda b, pt: (b, 0, 0))

    anyspec = pl.BlockSpec(memory_space=pl.ANY)
    grid_spec = pltpu.PrefetchScalarGridSpec(
        num_scalar_prefetch=1,
        grid=(nseq,),
        in_specs=[seq_spec((8, 256)), seq_spec((8, 256)), seq_spec((8, MLA_ROPE)),
                  seq_spec((1, 256)), seq_spec((1, 256)), seq_spec((1, 256)),
                  seq_spec((1, MLA_ROPE)), seq_spec((8, 1))] + [anyspec] * 5,
        out_specs=[seq_spec((8, 256)), seq_spec((8, 256))],
        scratch_shapes=[pltpu.VMEM((2, chunk, 256, PAGE), F32),
                        pltpu.VMEM((2, chunk, 256, PAGE), F32),
                        pltpu.VMEM((2, chunk, PAGE, MLA_KV_RANK), F32),
                        pltpu.VMEM((2, chunk, MLA_ROPE, PAGE), F32),
                        pltpu.VMEM((2, chunk, FOX_HEADS, PAGE), F32),
                        pltpu.SemaphoreType.DMA((2, 5))],
    )
    kern = functools.partial(_decode_kernel, n_pages=n_pages, chunk=chunk)
    return pl.pallas_call(
        kern,
        grid_spec=grid_spec,
        out_shape=[jax.ShapeDtypeStruct((nseq, 8, 256), F32)] * 2,
        compiler_params=_params(("arbitrary",)),
        name="paged_decode",
    )(page_table, qbd, ql, qpe, knew, vnew, cnew, pnew, lfnew, kt_pages, vt_pages, ckv_pages,
      kpet_pages, lft_pages)


def _win_kernel(qt_ref, ktn_ref, vtn_ref, logc_ref, kin_ref, vin_ref, ot_ref, kout_ref, vout_ref,
                *, heads, log_self):
    b = pl.program_id(1)
    nseq = qt_ref.shape[1]
    sel = lax.broadcasted_iota(jnp.int32, (heads * HEAD_DIM, nseq), 1) == b

    def column(ref):
        return jnp.sum(jnp.where(sel, ref[...], 0.0), axis=1, keepdims=True)

    qcol, kcol, vcol = column(qt_ref), column(ktn_ref), column(vtn_ref)
    w = kin_ref.shape[3]
    lane = lax.broadcasted_iota(jnp.int32, (HEAD_DIM, w), 1)
    logc = logc_ref[...]
    cols = []
    for hh in range(heads):
        sl = slice(hh * HEAD_DIM, (hh + 1) * HEAD_DIM)
        q, kn, vn = qcol[sl], kcol[sl], vcol[sl]
        kin, vin = kin_ref[0, hh], vin_ref[0, hh]
        s = jnp.sum(kin * q, axis=0, keepdims=True) + logc
        s_new = jnp.sum(q * kn, axis=0, keepdims=True) + log_self
        m = jnp.maximum(jnp.max(s, axis=1, keepdims=True), s_new)
        p = jnp.exp(s - m)
        p_new = jnp.exp(s_new - m)
        l = jnp.sum(p, axis=1, keepdims=True) + p_new
        o = (jnp.sum(vin * p, axis=1, keepdims=True) + p_new * vn) / l
        cols.append(o)
        kout_ref[0, hh] = jnp.where(lane == w - 1, kn, pltpu.roll(kin, w - 1, 1))
        vout_ref[0, hh] = jnp.where(lane == w - 1, vn, pltpu.roll(vin, w - 1, 1))
    ocol = jnp.concatenate(cols, axis=0)

    @pl.when(b == 0)
    def _():
        ot_ref[...] = jnp.zeros_like(ot_ref)

    ot_ref[...] = jnp.where(sel, ocol, ot_ref[...])


def _win(qt, ktn, vtn, logc, kin, vin, *, heads, log_self):
    nseq, nh, d, w = kin.shape
    rows = heads * HEAD_DIM
    col_spec = pl.BlockSpec((rows, nseq), lambda hb, b: (hb, 0))
    buf_spec = pl.BlockSpec((1, heads, d, w), lambda hb, b: (b, hb, 0, 0))
    kern = functools.partial(_win_kernel, heads=heads, log_self=log_self)
    return pl.pallas_call(
        kern,
        grid=(nh // heads, nseq),
        in_specs=[col_spec, col_spec, col_spec, _const_spec(logc.shape), buf_spec, buf_spec],
        out_specs=[col_spec, buf_spec, buf_spec],
        out_shape=[jax.ShapeDtypeStruct((nh * d, nseq), F32),
                   jax.ShapeDtypeStruct(kin.shape, F32), jax.ShapeDtypeStruct(vin.shape, F32)],
        compiler_params=_params(("parallel", "arbitrary")),
        name="window_decode",
    )(qt, ktn, vtn, logc, kin, vin)


def _cos_sin(pos, half):
    inv = np.float32(ROPE_THETA) ** (-np.arange(half, dtype=np.float32) / np.float32(half))
    ang = pos.astype(np.float32)[:, None] * inv.astype(np.float32)[None, :]
    return np.cos(ang).astype(np.float32), np.sin(ang).astype(np.float32)


def _rope_tables_a(pos, scale):
    half = MLA_ROPE // 2
    cos, sin = _cos_sin(pos, half)
    t = pos.shape[0]
    one, zero = np.ones((t, MLA_NOPE), np.float32), np.zeros((t, MLA_NOPE), np.float32)
    z16, z32 = np.zeros((t, half), np.float32), np.zeros((t, 32), np.float32)
    cq = np.concatenate([one, cos, cos, z32], axis=1) * np.float32(scale)
    s1 = np.concatenate([zero, -sin, z16, z32], axis=1) * np.float32(scale)
    s2 = np.concatenate([zero, z16, sin, z32], axis=1) * np.float32(scale)
    return tuple(jnp.asarray(a) for a in (cq, s1, s2, cos.T, sin.T))


def _rope_tables_c(pos, scale):
    half = C_ROT // 2
    cos, sin = _cos_sin(pos, half)
    t = pos.shape[0]
    one = np.ones((t, HEAD_DIM - C_ROT), np.float32)
    zero = np.zeros((t, HEAD_DIM - C_ROT), np.float32)
    z8 = np.zeros((t, half), np.float32)
    cq = np.concatenate([cos, cos, one] * 2, axis=1) * np.float32(scale)
    s1 = np.concatenate([-sin, z8, zero] * 2, axis=1) * np.float32(scale)
    s2 = np.concatenate([z8, sin, zero] * 2, axis=1) * np.float32(scale)
    return tuple(jnp.asarray(a) for a in (cq, s1, s2, cos.T, sin.T))


def _band_table(tq, tk, nb_win):
    od = np.arange(-(tq // tk - 1), nb_win + 1)
    d = od[:, None, None] * tk + np.arange(tq)[None, :, None] - np.arange(tk)[None, None, :]
    cnt = np.zeros(d.shape, np.float64)
    for w, r in C_BRANCHES:
        cnt += (d >= 0) & (d <= w) & (d % r == 0)
    return jnp.asarray(np.where(cnt > 0, np.log2(np.maximum(cnt, 1.0)), NEG_INF), F32)


def _sample_logc(w):
    d = w - np.arange(w)
    cnt = np.zeros(d.shape, np.float64)
    for ww, r in C_BRANCHES:
        cnt += (d <= ww) & (d % r == 0)
    return jnp.asarray(np.where(cnt > 0, np.log(np.maximum(cnt, 1.0)), NEG_INF), F32)[None, :]


def _prep_a(w_in, fox_bf, gq, wqup, gkv, wuk, wuv, g_mix):
    o = np.cumsum((0, 512, 256, 256, 8, MLA_Q_RANK, MLA_KV_RANK, MLA_ROPE))
    wq, wk, wv, wf, wcq, wckv, wkpe = (w_in[:, o[t]:o[t + 1]] for t in range(7))
    wt = jnp.concatenate([wk, wv, wkpe, wf], axis=1).T.astype(BF16)
    up = wqup.reshape(MLA_Q_RANK, MLA_HEADS, MLA_NOPE + MLA_ROPE)
    up = jnp.pad(up, ((0, 0), (0, 0), (0, LANES - MLA_NOPE - MLA_ROPE)))
    wukt = jnp.transpose(wuk, (1, 2, 0)).reshape(MLA_HEADS * MLA_NOPE, MLA_KV_RANK)
    wuvt = jnp.transpose(wuv, (1, 2, 0)).reshape(MLA_HEADS * HEAD_DIM, MLA_KV_RANK)
    wukp = jnp.pad(jnp.transpose(wuk, (1, 2, 0)), ((0, 0), (0, LANES - MLA_NOPE), (0, 0)))
    wuv_h = jnp.transpose(wuv, (1, 0, 2))
    z = jnp.zeros_like(wuv_h)
    even = jnp.concatenate([wuv_h, z], axis=2)
    odd = jnp.concatenate([z, wuv_h], axis=2)
    wuvp = jnp.where((jnp.arange(MLA_HEADS) % 2 == 0)[:, None, None], even, odd)
    return dict(g=g_mix[None, :], wq=wq.astype(BF16), wcq=wcq.astype(BF16), wckv=wckv.astype(BF16),
                wt=wt, bf=fox_bf[:, None], gq=gq[None, :],
                wqup=up.reshape(MLA_Q_RANK, MLA_HEADS * LANES).astype(BF16), gkv=gkv[None, :],
                wukt=wukt.astype(BF16), wuvt=wuvt.astype(BF16), wukp=wukp.astype(BF16),
                wuvp=wuvp.astype(BF16))


def kernel(x_prompt, x_sample, cache_fox_k, cache_fox_v, cache_fox_logf, cache_mla_ckv, cache_mla_kpe, cache_win_k, cache_win_v, page_table, a_w_in, a_fox_bf, a_mla_gq, a_mla_wqup, a_mla_gkv, a_mla_wuk, a_mla_wuv, a_w_out, c_w_in, c_w_out, norm_mix, norm_ffn, ffn_w_up, ffn_w_down, norm_final):
    B, S, D = x_prompt.shape
    nseq = x_sample.shape[0]
    past_len = page_table.shape[1] * PAGE
    n_pool = cache_fox_k.shape[1]
    tq = PROJ_TILE
    ftq, ftk = FLASH_TQ, FLASH_TK
    pos_p = np.arange(S)
    pos_s = np.full((nseq,), past_len)

    wa = _prep_a(a_w_in[0], a_fox_bf[0], a_mla_gq[0], a_mla_wqup[0], a_mla_gkv[0], a_mla_wuk[0],
                 a_mla_wuv[0], norm_mix[0])
    wo_a = a_w_out[0].astype(BF16)
    gf0, wup0, wdn0 = norm_ffn[0][None, :], ffn_w_up[0].astype(BF16), ffn_w_down[0].astype(BF16)

    (qf, qm, kt, vt, lft, ckv, kpet, kfp, vf2, kmp, vm) = _proj_a(
        x_prompt, wa, _rope_tables_a(pos_p, MLA_SCALE * LOG2E), tq=tq)
    rct = _suffix(lft)
    o_fox = _flash(qf, kfp, vf2, rct.reshape(B * FOX_KV, 2, S), None, n_pairs=FOX_KV,
                   q_per_head=False, v_per_head=False, tq=ftq, tk=ftk)
    o_mla = _flash(qm, kmp, vm, None, None, n_pairs=MLA_HEADS // 2, q_per_head=True,
                   v_per_head=True, tq=ftq,
                   tk=ftk)
    xp = _out_mlp(x_prompt.reshape(B * S, D), [o_fox.reshape(B * S, -1), o_mla.reshape(B * S, -1)],
                  wo_a, gf0, wup0, wdn0, None, tm=512)

    p_fox_k = jnp.transpose(kt.reshape(B, FOX_KV, HEAD_DIM, S), (0, 3, 1, 2))[None]
    p_fox_v = jnp.transpose(vt.reshape(B, FOX_KV, HEAD_DIM, S), (0, 3, 1, 2))[None]
    p_fox_logf = jnp.transpose(lft, (0, 2, 1))[None]
    p_mla_ckv = ckv[None]
    p_mla_kpe = jnp.transpose(kpet, (0, 2, 1))[None]

    xs = x_sample.reshape(1, nseq, D)
    (qf_s, qm_s, kt_s, vt_s, lft_s, ckv_s, kpet_s, _, _, _, _) = _proj_a(
        xs, wa, _rope_tables_a(pos_s, MLA_SCALE * LOG2E), tq=nseq)
    qf_s = qf_s[0].astype(F32).reshape(nseq, FOX_KV, 2, HEAD_DIM)
    eye = jnp.eye(FOX_KV, dtype=F32)
    qbd = jnp.einsum("bkgd,kj->bkgjd", qf_s, eye).reshape(nseq, FOX_HEADS, FOX_KV * HEAD_DIM)
    ql = _qlat(qm_s[0], wa["wukp"]).reshape(nseq, MLA_HEADS, MLA_KV_RANK)
    qpe = qm_s[0].reshape(nseq, MLA_HEADS, LANES)[:, :, MLA_NOPE:MLA_NOPE + MLA_ROPE]
    knew = kt_s[0].T[:, None, :]
    vnew = vt_s[0].T[:, None, :]
    cnew = ckv_s[0][:, None, :]
    pnew = kpet_s[0].T[:, None, :]
    lfnew = lft_s[0].T[:, :, None]

    kt_pages = jnp.transpose(cache_fox_k[0], (0, 2, 3, 1)).reshape(n_pool, 256, PAGE)
    vt_pages = jnp.transpose(cache_fox_v[0], (0, 2, 3, 1)).reshape(n_pool, 256, PAGE)
    lft_pages = jnp.transpose(cache_fox_logf[0], (0, 2, 1))
    kpet_pages = jnp.transpose(cache_mla_kpe[0], (0, 2, 1))
    of_full, o_lat = _decode(page_table, qbd, ql, qpe, knew, vnew, cnew, pnew, lfnew, kt_pages,
                             vt_pages, cache_mla_ckv[0], kpet_pages, lft_pages, chunk=DECODE_CHUNK)
    of_full = of_full.reshape(nseq, FOX_KV, 2, FOX_KV, HEAD_DIM)
    o_fox_s = jnp.einsum("bkgjd,kj->bkgd", of_full, eye).reshape(nseq, FOX_HEADS * HEAD_DIM)
    o_mla_s = _omla(o_lat.reshape(nseq, -1), wa["wuvp"])
    xs2 = _out_mlp(x_sample.reshape(nseq, D), [o_fox_s, o_mla_s], wo_a, gf0, wup0, wdn0, None,
                   tm=nseq)

    s_fox_k = kt_s[0].T.reshape(1, nseq, 1, FOX_KV, HEAD_DIM)
    s_fox_v = vt_s[0].T.reshape(1, nseq, 1, FOX_KV, HEAD_DIM)
    s_fox_logf = lft_s[0].T.reshape(1, nseq, 1, FOX_HEADS)
    s_mla_ckv = ckv_s.reshape(1, nseq, 1, MLA_KV_RANK)
    s_mla_kpe = kpet_s[0].T.reshape(1, nseq, 1, MLA_ROPE)

    n = C_HEADS * HEAD_DIM
    wc = c_w_in[0]
    wo_c = c_w_out[0].astype(BF16)
    g1 = norm_mix[1][None, :]
    gf1, wup1, wdn1 = norm_ffn[1][None, :], ffn_w_up[1].astype(BF16), ffn_w_down[1].astype(BF16)
    gfin = norm_final[None, :]
    nb_win = C_WIN // ftk
    qc, kcp, vc, ktf, vtf = _proj_c_prompt(
        xp.reshape(B, S, D), g1, wc[:, :n].astype(BF16), wc[:, n:].T.astype(BF16),
        _rope_tables_c(pos_p, C_SCALE * LOG2E), tq=tq)
    o_c = _flash(qc, kcp, vc, None, _band_table(ftq, ftk, nb_win), n_pairs=C_HEADS // 2,
                 q_per_head=False, v_per_head=True, tq=ftq, tk=ftk, nb_win=nb_win)
    y_prompt = _out_mlp(xp, [o_c.reshape(B * S, n)], wo_c, gf1, wup1, wdn1, gfin, tm=512)
    y_prompt = y_prompt.reshape(B, S, D)
    keep = ktf.shape[2]
    p_win_k = jnp.transpose(ktf.reshape(B, C_HEADS, HEAD_DIM, keep), (0, 3, 1, 2))[None]
    p_win_v = jnp.transpose(vtf.reshape(B, C_HEADS, HEAD_DIM, keep), (0, 3, 1, 2))[None]

    _, _, _, ct_s, st_s = _rope_tables_c(pos_s, 1.0)
    qt_s, ktn_s, vtn_s = _proj_c_sample(xs2, g1, wc.T.astype(BF16), ct_s, st_s)
    kin = jnp.transpose(cache_win_k[0], (0, 2, 3, 1))
    vin = jnp.transpose(cache_win_v[0], (0, 2, 3, 1))
    w = kin.shape[3]
    ot_s, kout, vout = _win(qt_s, ktn_s, vtn_s, _sample_logc(w), kin, vin, heads=WIN_HEADS,
                            log_self=math.log(len(C_BRANCHES)))
    y_sample = _out_mlp(xs2, [ot_s.T], wo_c, gf1, wup1, wdn1, gfin, tm=nseq)
    y_sample = y_sample.reshape(nseq, 1, D)
    s_win_k = jnp.transpose(kout, (0, 3, 1, 2))[None]
    s_win_v = jnp.transpose(vout, (0, 3, 1, 2))[None]

    return (y_prompt, y_sample, p_fox_k, p_fox_v, p_fox_logf, p_mla_ckv, p_mla_kpe, p_win_k, p_win_v,
            s_fox_k, s_fox_v, s_fox_logf, s_mla_ckv, s_mla_kpe, s_win_k, s_win_v)
```
